```python
import math
import jax, jax.numpy as jnp
from jax import lax
import numpy as np

D_MODEL = 2048
BATCH = 1
SEQ = 16384
DEPTH = 2

N_A_LAYERS = DEPTH // 2
N_B_LAYERS = DEPTH - N_A_LAYERS
BLK = 128
ROPE_THETA = 500000.0
NORM_EPS = 1e-6

A_HEAD_DIM = 128
A_HEADS = D_MODEL // A_HEAD_DIM
A_ROT_DIM = A_HEAD_DIM // 4
A_GROUPS = ((128, 1), (512, 4), (2048, 16))
N_A_GROUPS = len(A_GROUPS)
A_GROUP_WIDTH = A_HEADS * A_HEAD_DIM

B_HEADS = D_MODEL // 128
B_NOPE_DIM = 128
B_ROPE_DIM = 64
B_V_DIM = 128
B_Q_LORA = 512
B_KV_LORA = 512

FFN_HIDDEN = -(-(8 * D_MODEL) // (3 * 256)) * 256

kernel_name = 'yoco_dilated_mla_hybrid'


def rms_norm(x, g):
    x32 = x.astype(jnp.float32)
    y = x32 * lax.rsqrt(jnp.mean(x32 * x32, axis=-1, keepdims=True) + NORM_EPS)
    return (y * g.astype(jnp.float32)).astype(x.dtype)


def rope_angles(seq_len, dim):
    inv_freq = ROPE_THETA ** (-jnp.arange(0, dim, 2, dtype=jnp.float32) / dim)
    ang = jnp.arange(seq_len, dtype=jnp.float32)[:, None] * inv_freq[None, :]
    return jnp.cos(ang), jnp.sin(ang)


def apply_rope(x, cos, sin):
    half = x.shape[-1] // 2
    x32 = x.astype(jnp.float32)
    x1, x2 = x32[..., :half], x32[..., half:]
    c = cos[None, :, None, :]
    s = sin[None, :, None, :]
    return jnp.concatenate([x1 * c - x2 * s, x2 * c + x1 * s], axis=-1).astype(x.dtype)


def partial_rope(x, cos, sin):
    return jnp.concatenate([apply_rope(x[..., :A_ROT_DIM], cos, sin), x[..., A_ROT_DIM:]], axis=-1)


def dilated_window_branch(q, k, v, window, dilation):
    b, s, h, e = q.shape
    span = window // dilation
    n_prev = -(-span // BLK)
    chunk = dilation * BLK
    sp = -(-s // chunk) * chunk
    m = sp // dilation
    nb = m // BLK

    def to_blocks(t):
        t = jnp.pad(t, ((0, 0), (0, sp - s), (0, 0), (0, 0)))
        t = t.reshape(b, m, dilation, h, e).transpose(0, 2, 1, 3, 4)
        return t.reshape(b, dilation, nb, BLK, h, e)

    def band(t):
        tp = jnp.pad(t, ((0, 0), (0, 0), (n_prev, 0), (0, 0), (0, 0), (0, 0)))
        return jnp.concatenate([tp[:, :, j:j + nb] for j in range(n_prev + 1)], axis=3)

    qb = to_blocks(q)
    kband = band(to_blocks(k))
    vband = band(to_blocks(v))
    kb_len = (n_prev + 1) * BLK
    qi = jnp.arange(BLK)[:, None]
    kj = jnp.arange(kb_len)[None, :]
    dist = qi + n_prev * BLK - kj
    key_sub = (jnp.arange(nb)[:, None, None] - n_prev) * BLK + kj[None]
    valid = (dist >= 0)[None] & (dist <= span)[None] & (key_sub >= 0)

    scale = e ** -0.5
    sc = jnp.einsum('bdnqhe,bdnkhe->bdnhqk', qb, kband, preferred_element_type=jnp.float32) * scale
    sc = jnp.where(valid[None, None, :, None], sc, -jnp.inf)
    lse = jax.nn.logsumexp(sc, axis=-1)
    p = jnp.exp(sc - lse[..., None])
    out = jnp.einsum('bdnhqk,bdnkhe->bdnqhe', p.astype(v.dtype), vband, preferred_element_type=jnp.float32)
    out = out.reshape(b, dilation, m, h, e).transpose(0, 2, 1, 3, 4).reshape(b, sp, h, e)[:, :s]
    lse = lse.transpose(0, 1, 2, 4, 3).reshape(b, dilation, m, h).transpose(0, 2, 1, 3).reshape(b, sp, h)[:, :s]
    return out, lse


def dilated_attention(xn, w_qkv, w_o, cos, sin):
    b, s, _ = xn.shape
    qkv = (xn @ w_qkv).reshape(b, s, N_A_GROUPS, 3, A_HEADS, A_HEAD_DIM)
    outs, lses = [], []
    for g, (window, dilation) in enumerate(A_GROUPS):
        q = partial_rope(qkv[:, :, g, 0], cos, sin)
        k = partial_rope(qkv[:, :, g, 1], cos, sin)
        o, l = dilated_window_branch(q, k, qkv[:, :, g, 2], window, dilation)
        outs.append(o)
        lses.append(l)
    wts = jax.nn.softmax(jnp.stack(lses), axis=0)
    o = jnp.sum(wts[..., None] * jnp.stack(outs), axis=0)
    return o.astype(xn.dtype).reshape(b, s, A_GROUP_WIDTH) @ w_o


def mla_shared_kv(h, kv_norm_g, w_kv_a, kv_a_norm_g, w_kv_b, cos, sin):
    b, s, _ = h.shape
    ckv = rms_norm(h, kv_norm_g) @ w_kv_a
    c = rms_norm(ckv[..., :B_KV_LORA], kv_a_norm_g)
    k_pe = apply_rope(ckv[..., None, B_KV_LORA:], cos, sin)[:, :, 0]
    kv = (c @ w_kv_b).reshape(b, s, B_HEADS, B_NOPE_DIM + B_V_DIM)
    return kv[..., :B_NOPE_DIM], k_pe, kv[..., B_NOPE_DIM:]


def mla_attention(xn, k_nope, k_pe, v, w_q_a, q_a_norm_g, w_q_b, w_o, cos, sin):
    b, s, _ = xn.shape
    cq = rms_norm(xn @ w_q_a, q_a_norm_g)
    q = (cq @ w_q_b).reshape(b, s, B_HEADS, B_NOPE_DIM + B_ROPE_DIM)
    q_nope = q[..., :B_NOPE_DIM]
    q_pe = apply_rope(q[..., B_NOPE_DIM:], cos, sin)
    nb = s // BLK
    scale = (B_NOPE_DIM + B_ROPE_DIM) ** -0.5
    key_pos = jnp.arange(s)

    def to_blocks(t):
        return t.reshape(b, nb, BLK, *t.shape[2:]).swapaxes(0, 1)

    def block_attn(args):
        qn, qp, i = args
        sc = (jnp.einsum('bqhe,bkhe->bhqk', qn, k_nope, preferred_element_type=jnp.float32)
              + jnp.einsum('bqhr,bkr->bhqk', qp, k_pe, preferred_element_type=jnp.float32)) * scale
        q_pos = i * BLK + jnp.arange(BLK)
        sc = jnp.where(key_pos[None, :] <= q_pos[:, None], sc, -jnp.inf)
        p = jax.nn.softmax(sc, axis=-1)
        return jnp.einsum('bhqk,bkhe->bqhe', p.astype(v.dtype), v, preferred_element_type=jnp.float32).astype(v.dtype)

    o = lax.map(block_attn, (to_blocks(q_nope), to_blocks(q_pe), jnp.arange(nb)))
    o = o.swapaxes(0, 1).reshape(b, s, B_HEADS * B_V_DIM)
    return o @ w_o


def swiglu(xn, w_gu, w_down):
    g, u = jnp.split(xn @ w_gu, 2, axis=-1)
    return (jax.nn.silu(g) * u) @ w_down


def setup_inputs(seed: int = 0) -> dict:
    key = jax.random.key(seed)
    ks = jax.random.split(key, 16)

    def w(k, shape, fan_in):
        return jax.random.normal(k, shape, jnp.float32) * fan_in ** -0.5

    def gain(k, shape):
        return 1.0 + 0.02 * jax.random.normal(k, shape, jnp.float32)

    a_cols = N_A_GROUPS * 3 * A_GROUP_WIDTH
    return {
        'x': jax.random.normal(ks[0], (BATCH, SEQ, D_MODEL), jnp.float32),
        'attn_norm_g': gain(ks[1], (DEPTH, D_MODEL)),
        'ffn_norm_g': gain(ks[2], (DEPTH, D_MODEL)),
        'a_w_qkv': w(ks[3], (N_A_LAYERS, D_MODEL, a_cols), D_MODEL),
        'a_w_o': w(ks[4], (N_A_LAYERS, A_GROUP_WIDTH, D_MODEL), A_GROUP_WIDTH),
        'kv_norm_g': gain(ks[5], (D_MODEL,)),
        'b_w_kv_a': w(ks[6], (D_MODEL, B_KV_LORA + B_ROPE_DIM), D_MODEL),
        'b_kv_a_norm_g': gain(ks[7], (B_KV_LORA,)),
        'b_w_kv_b': w(ks[8], (B_KV_LORA, B_HEADS * (B_NOPE_DIM + B_V_DIM)), B_KV_LORA),
        'b_w_q_a': w(ks[9], (N_B_LAYERS, D_MODEL, B_Q_LORA), D_MODEL),
        'b_q_a_norm_g': gain(ks[10], (N_B_LAYERS, B_Q_LORA)),
        'b_w_q_b': w(ks[11], (N_B_LAYERS, B_Q_LORA, B_HEADS * (B_NOPE_DIM + B_ROPE_DIM)), B_Q_LORA),
        'b_w_o': w(ks[12], (N_B_LAYERS, B_HEADS * B_V_DIM, D_MODEL), B_HEADS * B_V_DIM),
        'ffn_w_gu': w(ks[13], (DEPTH, D_MODEL, 2 * FFN_HIDDEN), D_MODEL),
        'ffn_w_down': w(ks[14], (DEPTH, FFN_HIDDEN, D_MODEL), FFN_HIDDEN),
        'final_norm_g': gain(ks[15], (D_MODEL,)),
    }


def reference(x, attn_norm_g, ffn_norm_g, a_w_qkv, a_w_o, kv_norm_g, b_w_kv_a, b_kv_a_norm_g, b_w_kv_b,
              b_w_q_a, b_q_a_norm_g, b_w_q_b, b_w_o, ffn_w_gu, ffn_w_down, final_norm_g):
    s = x.shape[1]
    cos_a, sin_a = rope_angles(s, A_ROT_DIM)
    cos_b, sin_b = rope_angles(s, B_ROPE_DIM)
    h = x
    k_nope = k_pe = v_shared = None
    for layer in range(DEPTH):
        if layer == N_A_LAYERS:
            k_nope, k_pe, v_shared = mla_shared_kv(h, kv_norm_g, b_w_kv_a, b_kv_a_norm_g, b_w_kv_b, cos_b, sin_b)
        xn = rms_norm(h, attn_norm_g[layer])
        if layer < N_A_LAYERS:
            h = h + dilated_attention(xn, a_w_qkv[layer], a_w_o[layer], cos_a, sin_a)
        else:
            j = layer - N_A_LAYERS
            h = h + mla_attention(xn, k_nope, k_pe, v_shared, b_w_q_a[j], b_q_a_norm_g[j], b_w_q_b[j], b_w_o[j], cos_b, sin_b)
        h = h + swiglu(rms_norm(h, ffn_norm_g[layer]), ffn_w_gu[layer], ffn_w_down[layer])
    return rms_norm(h, final_norm_g)
```

```python
import functools
import math

import jax
import jax.numpy as jnp
import numpy as np
from jax import lax
from jax.experimental import pallas as pl
from jax.experimental.pallas import tpu as pltpu

F32 = jnp.float32
BF16 = jnp.bfloat16

D_MODEL = 2048
ROPE_THETA = 500000.0
NORM_EPS = 1e-6
LANES = 128

A_HEAD_DIM = 128
A_HEADS = D_MODEL // A_HEAD_DIM
A_ROT_DIM = A_HEAD_DIM // 4
A_GROUPS = ((128, 1), (512, 4), (2048, 16))
A_WIDTH = A_HEADS * A_HEAD_DIM
A_SPAN_BLK = 128

B_HEADS = D_MODEL // 128
B_NOPE = 128
B_ROPE = 64
B_V = 128
B_Q_LORA = 512
B_KV_LORA = 512
B_QK_PAD = 256

FFN_HIDDEN = 5632

VMEM_LIMIT = 56 * 1024 * 1024

QKV_TM, QKV_TN = 1024, 1024
A_ATT_TQ = 256
A_WO_TM = 256
FFN_TM, FFN_TF = 512, 512
B_PROJ_TM = 256
B_ATT_TQ, B_ATT_TK = 1024, 1024
B_WO_TM = 512


def _params(*sem):
    return pltpu.CompilerParams(dimension_semantics=sem, vmem_limit_bytes=VMEM_LIMIT)


def _rms_scale(x):
    return x * lax.rsqrt(jnp.mean(x * x, axis=-1, keepdims=True) + NORM_EPS)


def _rope_tile(a, c, s_lo, s_hi, half):
    return (a * c + pltpu.roll(a, LANES - half, 1) * s_lo + pltpu.roll(a, half, 1) * s_hi)


def _a_qkv_kernel(x_ref, g_ref, w_ref, c_ref, slo_ref, shi_ref, o_ref, xn_ref, *slab, d, tm, tn):
    j = pl.program_id(1)

    @pl.when(j == 0)
    def _():
        xn_ref[...] = (_rms_scale(x_ref[...]) * g_ref[...]).astype(BF16)

    acc = jnp.dot(xn_ref[...], w_ref[...], preferred_element_type=F32)
    t = j // (A_WIDTH // tn)
    rot = t < 2
    qscale = jnp.where(t == 0, A_HEAD_DIM ** -0.5, 1.0).astype(F32)
    c = jnp.where(rot, c_ref[...], 1.0) * qscale
    s_lo = jnp.where(rot, slo_ref[...], 0.0) * qscale
    s_hi = jnp.where(rot, shi_ref[...], 0.0) * qscale
    for col in range(tn // LANES):
        sl = slice(col * LANES, (col + 1) * LANES)
        a = _rope_tile(acc[:, sl], c, s_lo, s_hi, A_ROT_DIM // 2)
        if d == 1:
            o_ref[0, :, sl] = a.astype(BF16)
        else:
            slab[0][col] = a
    if d > 1:
        for r in range(d):
            for col in range(tn // LANES):
                sl = slice(col * LANES, (col + 1) * LANES)
                o_ref[r, :, sl] = slab[0][col, pl.ds(r, tm // d, stride=d), :].astype(BF16)


def _a_qkv(x, gain, w_qkv, tabs, group, d):
    s = x.shape[0]
    tm, tn = QKV_TM, QKV_TN
    ncol = 3 * A_WIDTH // tn
    scratch = [pltpu.VMEM((tm, D_MODEL), BF16)]
    if d > 1:
        scratch.append(pltpu.VMEM((tn // LANES, tm, LANES), F32))
    tab_spec = pl.BlockSpec((tm, LANES), lambda i, j: (i, 0))
    return pl.pallas_call(
        functools.partial(_a_qkv_kernel, d=d, tm=tm, tn=tn),
        grid=(s // tm, ncol),
        in_specs=[
            pl.BlockSpec((tm, D_MODEL), lambda i, j: (i, 0)),
            pl.BlockSpec((1, D_MODEL), lambda i, j: (0, 0)),
            pl.BlockSpec((D_MODEL, tn), lambda i, j: (0, group * ncol + j)),
            tab_spec, tab_spec, tab_spec,
        ],
        out_specs=pl.BlockSpec((d, tm // d, tn), lambda i, j: (0, i, j)),
        out_shape=jax.ShapeDtypeStruct((d, s // d, 3 * A_WIDTH), BF16),
        scratch_shapes=scratch,
        compiler_params=_params("parallel", "arbitrary"),
        name=f"a_qkv_d{d}",
    )(x, gain, w_qkv, *tabs)


def _a_attn_kernel(q_ref, kp_ref, kc_ref, vp_ref, vc_ref, o_ref, lse_ref, *, tq):
    n = pl.program_id(1)
    span = A_SPAN_BLK
    ip = lax.broadcasted_iota(jnp.int32, (tq, span), 0)
    jp = lax.broadcasted_iota(jnp.int32, (tq, span), 1)
    valid_p = (jp >= ip) & (n > 0)
    ic = lax.broadcasted_iota(jnp.int32, (tq, tq), 0)
    jc = lax.broadcasted_iota(jnp.int32, (tq, tq), 1)
    valid_c = (jc <= ic) & (jc >= ic - span)
    lane = lax.broadcasted_iota(jnp.int32, (tq, LANES), 1)
    lse_tile = jnp.zeros((tq, LANES), F32)
    nt = (((1,), (1,)), ((), ()))
    for h in range(A_HEADS):
        sl = slice(h * A_HEAD_DIM, (h + 1) * A_HEAD_DIM)
        q = q_ref[:, sl]
        s_p = lax.dot_general(q, kp_ref[:, sl], nt, preferred_element_type=F32)
        s_c = lax.dot_general(q, kc_ref[:, sl], nt, preferred_element_type=F32)
        s_p = jnp.where(valid_p, s_p, -jnp.inf)
        s_c = jnp.where(valid_c, s_c, -jnp.inf)
        m = jnp.maximum(jnp.max(s_p, axis=-1, keepdims=True), jnp.max(s_c, axis=-1, keepdims=True))
        p_p = jnp.exp(s_p - m)
        p_c = jnp.exp(s_c - m)
        l = jnp.sum(p_p, axis=-1, keepdims=True) + jnp.sum(p_c, axis=-1, keepdims=True)
        o = (jnp.dot(p_p.astype(BF16), vp_ref[:, sl], preferred_element_type=F32)
             + jnp.dot(p_c.astype(BF16), vc_ref[:, sl], preferred_element_type=F32))
        o_ref[:, sl] = o / l
        lse_tile = jnp.where(lane == h, m + jnp.log(l), lse_tile)
    lse_ref[...] = lse_tile


def _a_attn(qkv, d):
    _, m, _ = qkv.shape
    tq = A_ATT_TQ
    ratio = tq // A_SPAN_BLK
    cur = lambda col: pl.BlockSpec((None, tq, A_WIDTH), lambda r, n: (r, n, col))
    prev = lambda col: pl.BlockSpec(
        (None, A_SPAN_BLK, A_WIDTH), lambda r, n: (r, jnp.maximum(n * ratio - 1, 0), col))
    return pl.pallas_call(
        functools.partial(_a_attn_kernel, tq=tq),
        grid=(d, m // tq),
        in_specs=[cur(0), prev(1), cur(1), prev(2), cur(2)],
        out_specs=[
            pl.BlockSpec((None, tq, A_WIDTH), lambda r, n: (r, n, 0)),
            pl.BlockSpec((None, tq, LANES), lambda r, n: (r, n, 0)),
        ],
        out_shape=[
            jax.ShapeDtypeStruct((d, m, A_WIDTH), F32),
            jax.ShapeDtypeStruct((d, m, LANES), F32),
        ],
        compiler_params=_params("parallel", "arbitrary"),
        name=f"a_attn_d{d}",
    )(qkv, qkv, qkv, qkv, qkv)


def _a_wo_kernel(o1_ref, o2_ref, o3_ref, l1_ref, l2_ref, l3_ref, h_ref, w_ref, out_ref,
                 nat2_ref, nat3_ref, ln2_ref, ln3_ref, xb_ref, *, tm):
    for o_ref, l_ref, nat_ref, ln_ref, (_, d) in (
            (o2_ref, l2_ref, nat2_ref, ln2_ref, A_GROUPS[1]),
            (o3_ref, l3_ref, nat3_ref, ln3_ref, A_GROUPS[2])):
        for r in range(d):
            rows = pl.ds(r, tm // d, stride=d)
            ln_ref[0, rows, :] = l_ref[r]
            for h in range(A_HEADS):
                nat_ref[h, rows, :] = o_ref[r, :, h * A_HEAD_DIM:(h + 1) * A_HEAD_DIM]
    l1, l2, l3 = l1_ref[0], ln2_ref[0], ln3_ref[0]
    mx = jnp.maximum(jnp.maximum(l1, l2), l3)
    e1, e2, e3 = jnp.exp(l1 - mx), jnp.exp(l2 - mx), jnp.exp(l3 - mx)
    den = e1 + e2 + e3
    w1, w2, w3 = e1 / den, e2 / den, e3 / den
    for h in range(A_HEADS):
        sl = slice(h * A_HEAD_DIM, (h + 1) * A_HEAD_DIM)
        merged = (w1[:, h:h + 1] * o1_ref[0, :, sl] + w2[:, h:h + 1] * nat2_ref[h]
                  + w3[:, h:h + 1] * nat3_ref[h])
        xb_ref[:, sl] = merged.astype(BF16)
    out_ref[...] = h_ref[...] + jnp.dot(xb_ref[...], w_ref[...], preferred_element_type=F32)


def _a_wo(outs, lses, h, w_o):
    s = h.shape[0]
    tm = A_WO_TM
    o_spec = lambda d: pl.BlockSpec((d, tm // d, A_WIDTH), lambda i: (0, i, 0))
    l_spec = lambda d: pl.BlockSpec((d, tm // d, LANES), lambda i: (0, i, 0))
    ds_ = [d for _, d in A_GROUPS]
    return pl.pallas_call(
        functools.partial(_a_wo_kernel, tm=tm),
        grid=(s // tm,),
        in_specs=[o_spec(ds_[0]), o_spec(ds_[1]), o_spec(ds_[2]),
                  l_spec(ds_[0]), l_spec(ds_[1]), l_spec(ds_[2]),
                  pl.BlockSpec((tm, D_MODEL), lambda i: (i, 0)),
                  pl.BlockSpec((A_WIDTH, D_MODEL), lambda i: (0, 0))],
        out_specs=pl.BlockSpec((tm, D_MODEL), lambda i: (i, 0)),
        out_shape=jax.ShapeDtypeStruct((s, D_MODEL), F32),
        scratch_shapes=[
            pltpu.VMEM((A_HEADS, tm, LANES), F32),
            pltpu.VMEM((A_HEADS, tm, LANES), F32),
            pltpu.VMEM((1, tm, LANES), F32),
            pltpu.VMEM((1, tm, LANES), F32),
            pltpu.VMEM((tm, A_WIDTH), BF16),
        ],
        compiler_params=_params("parallel"),
        name="a_wo",
    )(*outs, *lses, h, w_o)


def _ffn_kernel(h_ref, g_ref, wg_ref, wu_ref, wd_ref, fg_ref, out_ref, xn_ref, *, final_norm):
    f = pl.program_id(1)

    @pl.when(f == 0)
    def _():
        xn_ref[...] = (_rms_scale(h_ref[...]) * g_ref[...]).astype(BF16)

    xn = xn_ref[...]
    gate = jnp.dot(xn, wg_ref[...], preferred_element_type=F32)
    up = jnp.dot(xn, wu_ref[...], preferred_element_type=F32)
    act = (gate * jax.nn.sigmoid(gate) * up).astype(BF16)
    part = jnp.dot(act, wd_ref[...], preferred_element_type=F32)

    @pl.when(f == 0)
    def _():
        out_ref[...] = part

    @pl.when(f > 0)
    def _():
        out_ref[...] += part

    @pl.when(f == pl.num_programs(1) - 1)
    def _():
        y = h_ref[...] + out_ref[...]
        if final_norm:
            y = _rms_scale(y) * fg_ref[...]
        out_ref[...] = y


def _ffn(h, gain, w_gu, w_down, final_gain, final_norm):
    s = h.shape[0]
    tm, tf = FFN_TM, FFN_TF
    nf = FFN_HIDDEN // tf
    return pl.pallas_call(
        functools.partial(_ffn_kernel, final_norm=final_norm),
        grid=(s // tm, nf),
        in_specs=[
            pl.BlockSpec((tm, D_MODEL), lambda i, f: (i, 0)),
            pl.BlockSpec((1, D_MODEL), lambda i, f: (0, 0)),
            pl.BlockSpec((D_MODEL, tf), lambda i, f: (0, f)),
            pl.BlockSpec((D_MODEL, tf), lambda i, f: (0, nf + f)),
            pl.BlockSpec((tf, D_MODEL), lambda i, f: (f, 0)),
            pl.BlockSpec((1, D_MODEL), lambda i, f: (0, 0)),
        ],
        out_specs=pl.BlockSpec((tm, D_MODEL), lambda i, f: (i, 0)),
        out_shape=jax.ShapeDtypeStruct((s, D_MODEL), F32),
        scratch_shapes=[pltpu.VMEM((tm, D_MODEL), BF16)],
        compiler_params=_params("parallel", "arbitrary"),
        name="ffn_final" if final_norm else "ffn",
    )(h, gain, w_gu, w_gu, w_down, final_gain)


def _b_proj_kernel(h_ref, ga_ref, gk_ref, wqa_ref, wkva_ref, gqa_ref, gkva_ref, wqb_ref, wkvb_ref,
                   c_ref, slo_ref, shi_ref, q_ref, k_ref, v_ref):
    y = _rms_scale(h_ref[...])
    xa = (y * ga_ref[...]).astype(BF16)
    xk = (y * gk_ref[...]).astype(BF16)
    c, s_lo, s_hi = c_ref[...], slo_ref[...], shi_ref[...]

    cq = jnp.dot(xa, wqa_ref[...], preferred_element_type=F32)
    cq = (_rms_scale(cq) * gqa_ref[...]).astype(BF16)
    q = jnp.dot(cq, wqb_ref[...], preferred_element_type=F32)
    qscale = (B_NOPE + B_ROPE) ** -0.5
    for h in range(B_HEADS):
        lo = h * B_QK_PAD
        q_ref[:, lo:lo + B_NOPE] = (q[:, lo:lo + B_NOPE] * qscale).astype(BF16)
        pe = _rope_tile(q[:, lo + B_NOPE:lo + B_QK_PAD], c, s_lo, s_hi, B_ROPE // 2)
        q_ref[:, lo + B_NOPE:lo + B_QK_PAD] = (pe * qscale).astype(BF16)

    ckv = jnp.dot(xk, wkva_ref[...], preferred_element_type=F32)
    lat = (_rms_scale(ckv[:, :B_KV_LORA]) * gkva_ref[...]).astype(BF16)
    k_pe = _rope_tile(ckv[:, B_KV_LORA:], c, s_lo, s_hi, B_ROPE // 2).astype(BF16)
    kv = jnp.dot(lat, wkvb_ref[...], preferred_element_type=F32)
    for h in range(B_HEADS):
        lo = h * B_QK_PAD
        k_ref[:, lo:lo + B_NOPE] = kv[:, lo:lo + B_NOPE].astype(BF16)
        k_ref[:, lo + B_NOPE:lo + B_QK_PAD] = k_pe
        v_ref[:, h * B_V:(h + 1) * B_V] = kv[:, lo + B_NOPE:lo + B_QK_PAD].astype(BF16)


def _b_proj(h, g_attn, g_kv, w_q_a, w_kv_a, g_qa, g_kva, w_q_b, w_kv_b, tabs):
    s = h.shape[0]
    tm = B_PROJ_TM
    full = lambda a: pl.BlockSpec(a.shape, lambda i: (0,) * a.ndim)
    row = lambda width: pl.BlockSpec((tm, width), lambda i: (i, 0))
    return pl.pallas_call(
        _b_proj_kernel,
        grid=(s // tm,),
        in_specs=[row(D_MODEL), full(g_attn), full(g_kv), full(w_q_a), full(w_kv_a), full(g_qa),
                  full(g_kva), full(w_q_b), full(w_kv_b), row(LANES), row(LANES), row(LANES)],
        out_specs=[row(B_HEADS * B_QK_PAD), row(B_HEADS * B_QK_PAD), row(B_HEADS * B_V)],
        out_shape=[
            jax.ShapeDtypeStruct((s, B_HEADS * B_QK_PAD), BF16),
            jax.ShapeDtypeStruct((s, B_HEADS * B_QK_PAD), BF16),
            jax.ShapeDtypeStruct((s, B_HEADS * B_V), BF16),
        ],
        compiler_params=_params("parallel"),
        name="b_proj",
    )(h, g_attn, g_kv, w_q_a, w_kv_a, g_qa, g_kva, w_q_b, w_kv_b, *tabs)


def _b_attn_kernel(qi_ref, ki_ref, q_ref, k_ref, v_ref, o_ref, m_ref, l_ref, acc_ref, *, tq, tk):
    p = pl.program_id(1)
    qi, ki = qi_ref[p], ki_ref[p]

    @pl.when(ki == 0)
    def _():
        m_ref[...] = jnp.full(m_ref.shape, -jnp.inf, F32)
        l_ref[...] = jnp.zeros(l_ref.shape, F32)
        acc_ref[...] = jnp.zeros(acc_ref.shape, F32)

    def step(masked):
        s = lax.dot_general(q_ref[...], k_ref[...], (((1,), (1,)), ((), ())),
                            preferred_element_type=F32)
        if masked:
            q_pos = qi * tq + lax.broadcasted_iota(jnp.int32, (tq, tk), 0)
            k_pos = ki * tk + lax.broadcasted_iota(jnp.int32, (tq, tk), 1)
            s = jnp.where(k_pos <= q_pos, s, -jnp.inf)
        m_prev = m_ref[...]
        m_new = jnp.maximum(m_prev, jnp.max(s, axis=-1, keepdims=True))
        alpha = jnp.exp(m_prev - m_new)
        pr = jnp.exp(s - m_new)
        l_ref[...] = alpha * l_ref[...] + jnp.sum(pr, axis=-1, keepdims=True)
        acc_ref[...] = alpha * acc_ref[...] + jnp.dot(pr.astype(BF16), v_ref[...],
                                                      preferred_element_type=F32)
        m_ref[...] = m_new

    needs_mask = (ki + 1) * tk - 1 > qi * tq

    @pl.when(needs_mask)
    def _():
        step(True)

    @pl.when(jnp.logical_not(needs_mask))
    def _():
        step(False)

    @pl.when(ki == ((qi + 1) * tq - 1) // tk)
    def _():
        o_ref[...] = (acc_ref[...] / l_ref[...]).astype(o_ref.dtype)


def _b_attn(q, k, v):
    s = q.shape[0]
    tq, tk = B_ATT_TQ, B_ATT_TK
    pairs = [(a, b) for a in range(s // tq) for b in range(((a + 1) * tq - 1) // tk + 1)]
    qi = jnp.asarray(np.array([a for a, _ in pairs], np.int32))
    ki = jnp.asarray(np.array([b for _, b in pairs], np.int32))
    grid_spec = pltpu.PrefetchScalarGridSpec(
        num_scalar_prefetch=2,
        grid=(B_HEADS, len(pairs)),
        in_specs=[
            pl.BlockSpec((tq, B_QK_PAD), lambda h, p, qi, ki: (qi[p], h)),
            pl.BlockSpec((tk, B_QK_PAD), lambda h, p, qi, ki: (ki[p], h)),
            pl.BlockSpec((tk, B_V), lambda h, p, qi, ki: (ki[p], h)),
        ],
        out_specs=pl.BlockSpec((tq, B_V), lambda h, p, qi, ki: (qi[p], h)),
        scratch_shapes=[
            pltpu.VMEM((tq, 1), F32),
            pltpu.VMEM((tq, 1), F32),
            pltpu.VMEM((tq, B_V), F32),
        ],
    )
    return pl.pallas_call(
        functools.partial(_b_attn_kernel, tq=tq, tk=tk),
        grid_spec=grid_spec,
        out_shape=jax.ShapeDtypeStruct((s, B_HEADS * B_V), BF16),
        compiler_params=_params("parallel", "arbitrary"),
        name="b_attn",
    )(qi, ki, q, k, v)


def _b_wo_kernel(o_ref, h_ref, w_ref, out_ref):
    out_ref[...] = h_ref[...] + jnp.dot(o_ref[...], w_ref[...], preferred_element_type=F32)


def _b_wo(o, h, w_o):
    s = h.shape[0]
    tm = B_WO_TM
    return pl.pallas_call(
        _b_wo_kernel,
        grid=(s // tm,),
        in_specs=[pl.BlockSpec((tm, B_HEADS * B_V), lambda i: (i, 0)),
                  pl.BlockSpec((tm, D_MODEL), lambda i: (i, 0)),
                  pl.BlockSpec((B_HEADS * B_V, D_MODEL), lambda i: (0, 0))],
        out_specs=pl.BlockSpec((tm, D_MODEL), lambda i: (i, 0)),
        out_shape=jax.ShapeDtypeStruct((s, D_MODEL), F32),
        compiler_params=_params("parallel"),
        name="b_wo",
    )(o, h, w_o)


def _rope_tables(seq_len, dim, passthrough):
    inv_freq = ROPE_THETA ** (-jnp.arange(0, dim, 2, dtype=F32) / dim)
    ang = jnp.arange(seq_len, dtype=F32)[:, None] * inv_freq[None, :]
    cos, sin = jnp.cos(ang), jnp.sin(ang)
    half = dim // 2
    rest = LANES - dim
    tail = jnp.full((seq_len, rest), passthrough, F32)
    zeros = lambda n: jnp.zeros((seq_len, n), F32)
    c = jnp.concatenate([cos, cos, tail], axis=1)
    s_lo = jnp.concatenate([-sin, zeros(half + rest)], axis=1)
    s_hi = jnp.concatenate([zeros(half), sin, zeros(rest)], axis=1)
    return c, s_lo, s_hi


def kernel(x, attn_norm_g, ffn_norm_g, a_w_qkv, a_w_o, kv_norm_g, b_w_kv_a, b_kv_a_norm_g, b_w_kv_b,
           b_w_q_a, b_q_a_norm_g, b_w_q_b, b_w_o, ffn_w_gu, ffn_w_down, final_norm_g):
    assert x.shape[0] == 1 and x.shape[2] == D_MODEL
    s = x.shape[1]
    assert s % QKV_TM == 0 and s % (A_GROUPS[-1][1] * A_ATT_TQ) == 0 and s % B_ATT_TQ == 0
    h = x[0]
    row = lambda g: g.reshape(1, -1).astype(F32)
    tabs_a = _rope_tables(s, A_ROT_DIM, 1.0)
    tabs_b = _rope_tables(s, B_ROPE, 0.0)

    w_qkv = a_w_qkv[0].astype(BF16)
    outs, lses = [], []
    for group, (_, d) in enumerate(A_GROUPS):
        qkv = _a_qkv(h, row(attn_norm_g[0]), w_qkv, tabs_a, group, d)
        o, lse = _a_attn(qkv, d)
        outs.append(o)
        lses.append(lse)
    h = _a_wo(outs, lses, h, a_w_o[0].astype(BF16))
    h = _ffn(h, row(ffn_norm_g[0]), ffn_w_gu[0].astype(BF16), ffn_w_down[0].astype(BF16),
             row(final_norm_g), final_norm=False)

    w_q_b = b_w_q_b[0].reshape(B_Q_LORA, B_HEADS, B_NOPE + B_ROPE)
    w_q_b = jnp.pad(w_q_b, ((0, 0), (0, 0), (0, B_QK_PAD - B_NOPE - B_ROPE)))
    w_q_b = w_q_b.reshape(B_Q_LORA, B_HEADS * B_QK_PAD).astype(BF16)
    w_kv_a = jnp.pad(b_w_kv_a, ((0, 0), (0, LANES - B_ROPE))).astype(BF16)
    q, k, v = _b_proj(h, row(attn_norm_g[1]), row(kv_norm_g), b_w_q_a[0].astype(BF16), w_kv_a,
                      row(b_q_a_norm_g[0]), row(b_kv_a_norm_g), w_q_b, b_w_kv_b.astype(BF16), tabs_b)
    o = _b_attn(q, k, v)
    h = _b_wo(o, h, b_w_o[0].astype(BF16))
    h = _ffn(h, row(ffn_norm_g[1]), ffn_w_gu[1].astype(BF16), ffn_w_down[1].astype(BF16),
             row(final_norm_g), final_norm=True)
    return h[None]
```

```python
import functools
import math

import jax
import jax.numpy as jnp
import numpy as np
from jax import lax
from jax.experimental import pallas as pl
from jax.experimental.pallas import tpu as pltpu

F32 = jnp.float32
BF16 = jnp.bfloat16

D_MODEL = 2048
ROPE_THETA = 500000.0
NORM_EPS = 1e-6
LANES = 128

A_HEAD_DIM = 128
A_HEADS = D_MODEL // A_HEAD_DIM
A_ROT_DIM = A_HEAD_DIM // 4
A_GROUPS = ((128, 1), (512, 4), (2048, 16))
A_WIDTH = A_HEADS * A_HEAD_DIM
A_SPAN_BLK = 128

B_HEADS = D_MODEL // 128
B_NOPE = 128
B_ROPE = 64
B_V = 128
B_Q_LORA = 512
B_KV_LORA = 512
B_QK_PAD = 256

FFN_HIDDEN = 5632

VMEM_LIMIT = 56 * 1024 * 1024

QKV_TM, QKV_TN = 1024, 1024
A_ATT_TQ = 256
A_WO_TM = 256
FFN_TM, FFN_TF = 512, 512
B_PROJ_TM = 256
B_ATT_T, B_ATT_CQ = 2048, 512
B_WO_TM = 512


def _params(*sem):
    return pltpu.CompilerParams(dimension_semantics=sem, vmem_limit_bytes=VMEM_LIMIT)


def _rms_scale(x):
    return x * lax.rsqrt(jnp.mean(x * x, axis=-1, keepdims=True) + NORM_EPS)


def _rope_tile(a, c, s_lo, s_hi, half):
    return (a * c + pltpu.roll(a, LANES - half, 1) * s_lo + pltpu.roll(a, half, 1) * s_hi)


def _a_qkv_kernel(x_ref, g_ref, w_ref, c_ref, slo_ref, shi_ref, o_ref, xn_ref, *slab, d, tm, tn):
    j = pl.program_id(1)

    @pl.when(j == 0)
    def _():
        xn_ref[...] = (_rms_scale(x_ref[...]) * g_ref[...]).astype(BF16)

    acc = jnp.dot(xn_ref[...], w_ref[...], preferred_element_type=F32)
    t = j // (A_WIDTH // tn)
    rot = t < 2
    qscale = jnp.where(t == 0, A_HEAD_DIM ** -0.5, 1.0).astype(F32)
    c = jnp.where(rot, c_ref[...], 1.0) * qscale
    s_lo = jnp.where(rot, slo_ref[...], 0.0) * qscale
    s_hi = jnp.where(rot, shi_ref[...], 0.0) * qscale
    for col in range(tn // LANES):
        sl = slice(col * LANES, (col + 1) * LANES)
        a = _rope_tile(acc[:, sl], c, s_lo, s_hi, A_ROT_DIM // 2)
        if d == 1:
            o_ref[0, :, sl] = a.astype(BF16)
        else:
            slab[0][col] = a
    if d > 1:
        for r in range(d):
            for col in range(tn // LANES):
                sl = slice(col * LANES, (col + 1) * LANES)
                o_ref[r, :, sl] = slab[0][col, pl.ds(r, tm // d, stride=d), :].astype(BF16)


def _a_qkv(x, gain, w_qkv, tabs, group, d):
    s = x.shape[0]
    tm, tn = QKV_TM, QKV_TN
    ncol = 3 * A_WIDTH // tn
    scratch = [pltpu.VMEM((tm, D_MODEL), BF16)]
    if d > 1:
        scratch.append(pltpu.VMEM((tn // LANES, tm, LANES), F32))
    tab_spec = pl.BlockSpec((tm, LANES), lambda i, j: (i, 0))
    return pl.pallas_call(
        functools.partial(_a_qkv_kernel, d=d, tm=tm, tn=tn),
        grid=(s // tm, ncol),
        in_specs=[
            pl.BlockSpec((tm, D_MODEL), lambda i, j: (i, 0)),
            pl.BlockSpec((1, D_MODEL), lambda i, j: (0, 0)),
            pl.BlockSpec((D_MODEL, tn), lambda i, j: (0, group * ncol + j)),
            tab_spec, tab_spec, tab_spec,
        ],
        out_specs=pl.BlockSpec((d, tm // d, tn), lambda i, j: (0, i, j)),
        out_shape=jax.ShapeDtypeStruct((d, s // d, 3 * A_WIDTH), BF16),
        scratch_shapes=scratch,
        compiler_params=_params("parallel", "arbitrary"),
        name=f"a_qkv_d{d}",
    )(x, gain, w_qkv, *tabs)


def _a_attn_kernel(q_ref, kp_ref, kc_ref, vp_ref, vc_ref, o_ref, lse_ref, *, tq):
    n = pl.program_id(1)
    span = A_SPAN_BLK
    ip = lax.broadcasted_iota(jnp.int32, (tq, span), 0)
    jp = lax.broadcasted_iota(jnp.int32, (tq, span), 1)
    valid_p = (jp >= ip) & (n > 0)
    ic = lax.broadcasted_iota(jnp.int32, (tq, tq), 0)
    jc = lax.broadcasted_iota(jnp.int32, (tq, tq), 1)
    valid_c = (jc <= ic) & (jc >= ic - span)
    lane = lax.broadcasted_iota(jnp.int32, (tq, LANES), 1)
    lse_tile = jnp.zeros((tq, LANES), F32)
    nt = (((1,), (1,)), ((), ()))
    for h in range(A_HEADS):
        sl = slice(h * A_HEAD_DIM, (h + 1) * A_HEAD_DIM)
        q = q_ref[:, sl]
        s_p = lax.dot_general(q, kp_ref[:, sl], nt, preferred_element_type=F32)
        s_c = lax.dot_general(q, kc_ref[:, sl], nt, preferred_element_type=F32)
        s_p = jnp.where(valid_p, s_p, -jnp.inf)
        s_c = jnp.where(valid_c, s_c, -jnp.inf)
        m = jnp.maximum(jnp.max(s_p, axis=-1, keepdims=True), jnp.max(s_c, axis=-1, keepdims=True))
        p_p = jnp.exp(s_p - m)
        p_c = jnp.exp(s_c - m)
        l = jnp.sum(p_p, axis=-1, keepdims=True) + jnp.sum(p_c, axis=-1, keepdims=True)
        o = (jnp.dot(p_p.astype(BF16), vp_ref[:, sl], preferred_element_type=F32)
             + jnp.dot(p_c.astype(BF16), vc_ref[:, sl], preferred_element_type=F32))
        o_ref[:, sl] = o / l
        lse_tile = jnp.where(lane == h, m + jnp.log(l), lse_tile)
    lse_ref[...] = lse_tile


def _a_attn(qkv, d):
    _, m, _ = qkv.shape
    tq = A_ATT_TQ
    ratio = tq // A_SPAN_BLK
    cur = lambda col: pl.BlockSpec((None, tq, A_WIDTH), lambda r, n: (r, n, col))
    prev = lambda col: pl.BlockSpec(
        (None, A_SPAN_BLK, A_WIDTH), lambda r, n: (r, jnp.maximum(n * ratio - 1, 0), col))
    return pl.pallas_call(
        functools.partial(_a_attn_kernel, tq=tq),
        grid=(d, m // tq),
        in_specs=[cur(0), prev(1), cur(1), prev(2), cur(2)],
        out_specs=[
            pl.BlockSpec((None, tq, A_WIDTH), lambda r, n: (r, n, 0)),
            pl.BlockSpec((None, tq, LANES), lambda r, n: (r, n, 0)),
        ],
        out_shape=[
            jax.ShapeDtypeStruct((d, m, A_WIDTH), F32),
            jax.ShapeDtypeStruct((d, m, LANES), F32),
        ],
        compiler_params=_params("parallel", "arbitrary"),
        name=f"a_attn_d{d}",
    )(qkv, qkv, qkv, qkv, qkv)


def _a_wo_kernel(o1_ref, o2_ref, o3_ref, l1_ref, l2_ref, l3_ref, h_ref, w_ref, out_ref,
                 nat2_ref, nat3_ref, ln2_ref, ln3_ref, xb_ref, *, tm):
    for o_ref, l_ref, nat_ref, ln_ref, (_, d) in (
            (o2_ref, l2_ref, nat2_ref, ln2_ref, A_GROUPS[1]),
            (o3_ref, l3_ref, nat3_ref, ln3_ref, A_GROUPS[2])):
        for r in range(d):
            rows = pl.ds(r, tm // d, stride=d)
            ln_ref[0, rows, :] = l_ref[r]
            for h in range(A_HEADS):
                nat_ref[h, rows, :] = o_ref[r, :, h * A_HEAD_DIM:(h + 1) * A_HEAD_DIM]
    l1, l2, l3 = l1_ref[0], ln2_ref[0], ln3_ref[0]
    mx = jnp.maximum(jnp.maximum(l1, l2), l3)
    e1, e2, e3 = jnp.exp(l1 - mx), jnp.exp(l2 - mx), jnp.exp(l3 - mx)
    den = e1 + e2 + e3
    w1, w2, w3 = e1 / den, e2 / den, e3 / den
    for h in range(A_HEADS):
        sl = slice(h * A_HEAD_DIM, (h + 1) * A_HEAD_DIM)
        merged = (w1[:, h:h + 1] * o1_ref[0, :, sl] + w2[:, h:h + 1] * nat2_ref[h]
                  + w3[:, h:h + 1] * nat3_ref[h])
        xb_ref[:, sl] = merged.astype(BF16)
    out_ref[...] = h_ref[...] + jnp.dot(xb_ref[...], w_ref[...], preferred_element_type=F32)


def _a_wo(outs, lses, h, w_o):
    s = h.shape[0]
    tm = A_WO_TM
    o_spec = lambda d: pl.BlockSpec((d, tm // d, A_WIDTH), lambda i: (0, i, 0))
    l_spec = lambda d: pl.BlockSpec((d, tm // d, LANES), lambda i: (0, i, 0))
    ds_ = [d for _, d in A_GROUPS]
    return pl.pallas_call(
        functools.partial(_a_wo_kernel, tm=tm),
        grid=(s // tm,),
        in_specs=[o_spec(ds_[0]), o_spec(ds_[1]), o_spec(ds_[2]),
                  l_spec(ds_[0]), l_spec(ds_[1]), l_spec(ds_[2]),
                  pl.BlockSpec((tm, D_MODEL), lambda i: (i, 0)),
                  pl.BlockSpec((A_WIDTH, D_MODEL), lambda i: (0, 0))],
        out_specs=pl.BlockSpec((tm, D_MODEL), lambda i: (i, 0)),
        out_shape=jax.ShapeDtypeStruct((s, D_MODEL), F32),
        scratch_shapes=[
            pltpu.VMEM((A_HEADS, tm, LANES), F32),
            pltpu.VMEM((A_HEADS, tm, LANES), F32),
            pltpu.VMEM((1, tm, LANES), F32),
            pltpu.VMEM((1, tm, LANES), F32),
            pltpu.VMEM((tm, A_WIDTH), BF16),
        ],
        compiler_params=_params("parallel"),
        name="a_wo",
    )(*outs, *lses, h, w_o)


def _ffn_kernel(h_ref, g_ref, wg_ref, wu_ref, wd_ref, fg_ref, out_ref, xn_ref, *, final_norm):
    f = pl.program_id(1)

    @pl.when(f == 0)
    def _():
        xn_ref[...] = (_rms_scale(h_ref[...]) * g_ref[...]).astype(BF16)

    xn = xn_ref[...]
    gate = jnp.dot(xn, wg_ref[...], preferred_element_type=F32)
    up = jnp.dot(xn, wu_ref[...], preferred_element_type=F32)
    act = (gate * jax.nn.sigmoid(gate) * up).astype(BF16)
    part = jnp.dot(act, wd_ref[...], preferred_element_type=F32)

    @pl.when(f == 0)
    def _():
        out_ref[...] = part

    @pl.when(f > 0)
    def _():
        out_ref[...] += part

    @pl.when(f == pl.num_programs(1) - 1)
    def _():
        y = h_ref[...] + out_ref[...]
        if final_norm:
            y = _rms_scale(y) * fg_ref[...]
        out_ref[...] = y


def _ffn(h, gain, w_gu, w_down, final_gain, final_norm):
    s = h.shape[0]
    tm, tf = FFN_TM, FFN_TF
    nf = FFN_HIDDEN // tf
    return pl.pallas_call(
        functools.partial(_ffn_kernel, final_norm=final_norm),
        grid=(s // tm, nf),
        in_specs=[
            pl.BlockSpec((tm, D_MODEL), lambda i, f: (i, 0)),
            pl.BlockSpec((1, D_MODEL), lambda i, f: (0, 0)),
            pl.BlockSpec((D_MODEL, tf), lambda i, f: (0, f)),
            pl.BlockSpec((D_MODEL, tf), lambda i, f: (0, nf + f)),
            pl.BlockSpec((tf, D_MODEL), lambda i, f: (f, 0)),
            pl.BlockSpec((1, D_MODEL), lambda i, f: (0, 0)),
        ],
        out_specs=pl.BlockSpec((tm, D_MODEL), lambda i, f: (i, 0)),
        out_shape=jax.ShapeDtypeStruct((s, D_MODEL), F32),
        scratch_shapes=[pltpu.VMEM((tm, D_MODEL), BF16)],
        compiler_params=_params("parallel", "arbitrary"),
        name="ffn_final" if final_norm else "ffn",
    )(h, gain, w_gu, w_gu, w_down, final_gain)


def _b_proj_kernel(h_ref, ga_ref, gk_ref, wqa_ref, wkva_ref, gqa_ref, gkva_ref, wqb_ref, wkvb_ref,
                   c_ref, slo_ref, shi_ref, q_ref, k_ref, vt_ref):
    y = _rms_scale(h_ref[...])
    xa = (y * ga_ref[...]).astype(BF16)
    xk = (y * gk_ref[...]).astype(BF16)
    c, s_lo, s_hi = c_ref[...], slo_ref[...], shi_ref[...]

    cq = jnp.dot(xa, wqa_ref[...], preferred_element_type=F32)
    cq = (_rms_scale(cq) * gqa_ref[...]).astype(BF16)
    q = jnp.dot(cq, wqb_ref[...], preferred_element_type=F32)
    qscale = (B_NOPE + B_ROPE) ** -0.5 * math.log2(math.e)
    for h in range(B_HEADS):
        lo = h * B_QK_PAD
        q_ref[:, lo:lo + B_NOPE] = (q[:, lo:lo + B_NOPE] * qscale).astype(BF16)
        pe = _rope_tile(q[:, lo + B_NOPE:lo + B_QK_PAD], c, s_lo, s_hi, B_ROPE // 2)
        q_ref[:, lo + B_NOPE:lo + B_QK_PAD] = (pe * qscale).astype(BF16)

    ckv = jnp.dot(xk, wkva_ref[...], preferred_element_type=F32)
    lat = (_rms_scale(ckv[:, :B_KV_LORA]) * gkva_ref[...]).astype(BF16)
    k_pe = _rope_tile(ckv[:, B_KV_LORA:], c, s_lo, s_hi, B_ROPE // 2).astype(BF16)
    kv = jnp.dot(lat, wkvb_ref[...], preferred_element_type=F32)
    for h in range(B_HEADS):
        lo = h * B_QK_PAD
        k_ref[:, lo:lo + B_NOPE] = kv[:, lo:lo + B_NOPE].astype(BF16)
        k_ref[:, lo + B_NOPE:lo + B_QK_PAD] = k_pe
        vt_ref[h * B_V:(h + 1) * B_V, :] = kv[:, lo + B_NOPE:lo + B_QK_PAD].T.astype(BF16)


def _b_proj(h, g_attn, g_kv, w_q_a, w_kv_a, g_qa, g_kva, w_q_b, w_kv_b, tabs):
    s = h.shape[0]
    tm = B_PROJ_TM
    full = lambda a: pl.BlockSpec(a.shape, lambda i: (0,) * a.ndim)
    row = lambda width: pl.BlockSpec((tm, width), lambda i: (i, 0))
    return pl.pallas_call(
        _b_proj_kernel,
        grid=(s // tm,),
        in_specs=[row(D_MODEL), full(g_attn), full(g_kv), full(w_q_a), full(w_kv_a), full(g_qa),
                  full(g_kva), full(w_q_b), full(w_kv_b), row(LANES), row(LANES), row(LANES)],
        out_specs=[row(B_HEADS * B_QK_PAD), row(B_HEADS * B_QK_PAD),
                   pl.BlockSpec((B_HEADS * B_V, tm), lambda i: (0, i))],
        out_shape=[
            jax.ShapeDtypeStruct((s, B_HEADS * B_QK_PAD), BF16),
            jax.ShapeDtypeStruct((s, B_HEADS * B_QK_PAD), BF16),
            jax.ShapeDtypeStruct((B_HEADS * B_V, s), BF16),
        ],
        compiler_params=_params("parallel"),
        name="b_proj",
    )(h, g_attn, g_kv, w_q_a, w_kv_a, g_qa, g_kva, w_q_b, w_kv_b, *tabs)


def _b_attn_kernel(qi_ref, ki_ref, q_ref, k_ref, vt_ref, o_ref, m_ref, l_ref, acc_ref, *, t, cq):
    p = pl.program_id(1)
    qi, ki = qi_ref[p], ki_ref[p]
    nchunk = t // cq

    @pl.when(ki == 0)
    def _():
        m_ref[...] = jnp.full(m_ref.shape, -jnp.inf, F32)
        l_ref[...] = jnp.zeros(l_ref.shape, F32)
        acc_ref[...] = jnp.zeros(acc_ref.shape, F32)

    def step(diag):
        rows = lambda c: (c + 1) * cq if diag else t

        def scores(c):
            kr = rows(c)
            s = lax.dot_general(k_ref[0:kr, :], q_ref[c * cq:(c + 1) * cq, :],
                                (((1,), (1,)), ((), ())), preferred_element_type=F32)
            if diag:
                k_pos = lax.broadcasted_iota(jnp.int32, (kr, cq), 0)
                q_pos = c * cq + lax.broadcasted_iota(jnp.int32, (kr, cq), 1)
                s = jnp.where(k_pos <= q_pos, s, -jnp.inf)
            return s

        def update(c, s):
            m_prev = m_ref[c]
            m_new = jnp.maximum(m_prev, jnp.max(s, axis=0, keepdims=True))
            alpha = jnp.exp2(m_prev - m_new)
            pt = jnp.exp2(s - m_new)
            l_ref[c] = alpha * l_ref[c] + jnp.sum(pt, axis=0, keepdims=True)
            pv = jnp.dot(vt_ref[:, 0:rows(c)], pt.astype(BF16),
                         preferred_element_type=F32)
            acc_ref[c] = alpha * acc_ref[c] + pv
            m_ref[c] = m_new

        s_next = scores(0)
        for c in range(nchunk):
            s_cur = s_next
            if c + 1 < nchunk:
                s_next = scores(c + 1)
            update(c, s_cur)

    @pl.when(ki == qi)
    def _():
        step(True)
        for c in range(nchunk):
            o = acc_ref[c] / l_ref[c]
            o_ref[c * cq:(c + 1) * cq, :] = o.T.astype(o_ref.dtype)

    @pl.when(ki < qi)
    def _():
        step(False)


def _b_attn(q, k, vt):
    s = q.shape[0]
    t, cq = B_ATT_T, B_ATT_CQ
    tq = tk = t
    pairs = [(a, b) for a in range(s // t) for b in range(a + 1)]
    qi = jnp.asarray(np.array([a for a, _ in pairs], np.int32))
    ki = jnp.asarray(np.array([b for _, b in pairs], np.int32))
    grid_spec = pltpu.PrefetchScalarGridSpec(
        num_scalar_prefetch=2,
        grid=(B_HEADS, len(pairs)),
        in_specs=[
            pl.BlockSpec((tq, B_QK_PAD), lambda h, p, qi, ki: (qi[p], h)),
            pl.BlockSpec((tk, B_QK_PAD), lambda h, p, qi, ki: (ki[p], h)),
            pl.BlockSpec((B_V, tk), lambda h, p, qi, ki: (h, ki[p])),
        ],
        out_specs=pl.BlockSpec((tq, B_V), lambda h, p, qi, ki: (qi[p], h)),
        scratch_shapes=[
            pltpu.VMEM((t // cq, 1, cq), F32),
            pltpu.VMEM((t // cq, 1, cq), F32),
            pltpu.VMEM((t // cq, B_V, cq), F32),
        ],
    )
    return pl.pallas_call(
        functools.partial(_b_attn_kernel, t=t, cq=cq),
        grid_spec=grid_spec,
        out_shape=jax.ShapeDtypeStruct((s, B_HEADS * B_V), BF16),
        compiler_params=_params("parallel", "arbitrary"),
        name="b_attn",
    )(qi, ki, q, k, vt)


def _b_wo_kernel(o_ref, h_ref, w_ref, out_ref):
    out_ref[...] = h_ref[...] + jnp.dot(o_ref[...], w_ref[...], preferred_element_type=F32)


def _b_wo(o, h, w_o):
    s = h.shape[0]
    tm = B_WO_TM
    return pl.pallas_call(
        _b_wo_kernel,
        grid=(s // tm,),
        in_specs=[pl.BlockSpec((tm, B_HEADS * B_V), lambda i: (i, 0)),
                  pl.BlockSpec((tm, D_MODEL), lambda i: (i, 0)),
                  pl.BlockSpec((B_HEADS * B_V, D_MODEL), lambda i: (0, 0))],
        out_specs=pl.BlockSpec((tm, D_MODEL), lambda i: (i, 0)),
        out_shape=jax.ShapeDtypeStruct((s, D_MODEL), F32),
        compiler_params=_params("parallel"),
        name="b_wo",
    )(o, h, w_o)


def _rope_tables(seq_len, dim, passthrough):
    inv_freq = ROPE_THETA ** (-jnp.arange(0, dim, 2, dtype=F32) / dim)
    ang = jnp.arange(seq_len, dtype=F32)[:, None] * inv_freq[None, :]
    cos, sin = jnp.cos(ang), jnp.sin(ang)
    half = dim // 2
    rest = LANES - dim
    tail = jnp.full((seq_len, rest), passthrough, F32)
    zeros = lambda n: jnp.zeros((seq_len, n), F32)
    c = jnp.concatenate([cos, cos, tail], axis=1)
    s_lo = jnp.concatenate([-sin, zeros(half + rest)], axis=1)
    s_hi = jnp.concatenate([zeros(half), sin, zeros(rest)], axis=1)
    return c, s_lo, s_hi


def kernel(x, attn_norm_g, ffn_norm_g, a_w_qkv, a_w_o, kv_norm_g, b_w_kv_a, b_kv_a_norm_g, b_w_kv_b,
           b_w_q_a, b_q_a_norm_g, b_w_q_b, b_w_o, ffn_w_gu, ffn_w_down, final_norm_g):
    assert x.shape[0] == 1 and x.shape[2] == D_MODEL
    s = x.shape[1]
    assert s % QKV_TM == 0 and s % (A_GROUPS[-1][1] * A_ATT_TQ) == 0 and s % B_ATT_T == 0
    h = x[0]
    row = lambda g: g.reshape(1, -1).astype(F32)
    tabs_a = _rope_tables(s, A_ROT_DIM, 1.0)
    tabs_b = _rope_tables(s, B_ROPE, 0.0)

    w_qkv = a_w_qkv[0].astype(BF16)
    outs, lses = [], []
    for group, (_, d) in enumerate(A_GROUPS):
        qkv = _a_qkv(h, row(attn_norm_g[0]), w_qkv, tabs_a, group, d)
        o, lse = _a_attn(qkv, d)
        outs.append(o)
        lses.append(lse)
    h = _a_wo(outs, lses, h, a_w_o[0].astype(BF16))
    h = _ffn(h, row(ffn_norm_g[0]), ffn_w_gu[0].astype(BF16), ffn_w_down[0].astype(BF16),
             row(final_norm_g), final_norm=False)

    w_q_b = b_w_q_b[0].reshape(B_Q_LORA, B_HEADS, B_NOPE + B_ROPE)
    w_q_b = jnp.pad(w_q_b, ((0, 0), (0, 0), (0, B_QK_PAD - B_NOPE - B_ROPE)))
    w_q_b = w_q_b.reshape(B_Q_LORA, B_HEADS * B_QK_PAD).astype(BF16)
    w_kv_a = jnp.pad(b_w_kv_a, ((0, 0), (0, LANES - B_ROPE))).astype(BF16)
    q, k, vt = _b_proj(h, row(attn_norm_g[1]), row(kv_norm_g), b_w_q_a[0].astype(BF16), w_kv_a,
                       row(b_q_a_norm_g[0]), row(b_kv_a_norm_g), w_q_b, b_w_kv_b.astype(BF16), tabs_b)
    o = _b_attn(q, k, vt)
    h = _b_wo(o, h, b_w_o[0].astype(BF16))
    h = _ffn(h, row(ffn_norm_g[1]), ffn_w_gu[1].astype(BF16), ffn_w_down[1].astype(BF16),
             row(final_norm_g), final_norm=True)
    return h[None]
```

```python
import functools
import math

import jax
import jax.numpy as jnp
import numpy as np
from jax import lax
from jax.experimental import pallas as pl
from jax.experimental.pallas import tpu as pltpu

F32 = jnp.float32
BF16 = jnp.bfloat16

D_MODEL = 2048
ROPE_THETA = 500000.0
NORM_EPS = 1e-6
LANES = 128

A_HEAD_DIM = 128
A_HEADS = D_MODEL // A_HEAD_DIM
A_ROT_DIM = A_HEAD_DIM // 4
A_GROUPS = ((128, 1), (512, 4), (2048, 16))
A_WIDTH = A_HEADS * A_HEAD_DIM
A_SPAN_BLK = 128

B_HEADS = D_MODEL // 128
B_NOPE = 128
B_ROPE = 64
B_V = 128
B_VE = B_V + 16
B_Q_LORA = 512
B_KV_LORA = 512
B_QK_PAD = 256

FFN_HIDDEN = 5632

VMEM_LIMIT = 56 * 1024 * 1024

QKV_TM, QKV_TN = 1024, 1024
A_ATT_TQ = 512
A_WO_TM = 256
FFN_TM, FFN_TF = 512, 512
B_PROJ_TM = 256
B_ATT_T, B_ATT_CQ = 2048, 512
B_ATT_KB = 256
B_WO_TM = 512


def _params(*sem, flags=None):
    return pltpu.CompilerParams(dimension_semantics=sem, vmem_limit_bytes=VMEM_LIMIT, flags=flags)


def _rms_scale(x):
    return x * lax.rsqrt(jnp.mean(x * x, axis=-1, keepdims=True) + NORM_EPS)


def _rope_tile(a, c, s_lo, s_hi, half):
    return (a * c + pltpu.roll(a, LANES - half, 1) * s_lo + pltpu.roll(a, half, 1) * s_hi)


def _a_qkv_kernel(x_ref, g_ref, w_ref, c_ref, slo_ref, shi_ref, o_ref, xn_ref, *slab, d, tm, tn):
    j = pl.program_id(1)

    @pl.when(j == 0)
    def _():
        xn_ref[...] = (_rms_scale(x_ref[...]) * g_ref[...]).astype(BF16)

    acc = jnp.dot(xn_ref[...], w_ref[...], preferred_element_type=F32)
    t = j // (A_WIDTH // tn)
    rot = t < 2
    qscale = jnp.where(t == 0, A_HEAD_DIM ** -0.5 * math.log2(math.e), 1.0).astype(F32)
    c = jnp.where(rot, c_ref[...], 1.0) * qscale
    s_lo = jnp.where(rot, slo_ref[...], 0.0) * qscale
    s_hi = jnp.where(rot, shi_ref[...], 0.0) * qscale
    for col in range(tn // LANES):
        sl = slice(col * LANES, (col + 1) * LANES)
        a = _rope_tile(acc[:, sl], c, s_lo, s_hi, A_ROT_DIM // 2)
        if d == 1:
            o_ref[0, :, sl] = a.astype(BF16)
        else:
            slab[0][col] = a
    if d > 1:
        for r in range(d):
            for col in range(tn // LANES):
                sl = slice(col * LANES, (col + 1) * LANES)
                o_ref[r, :, sl] = slab[0][col, pl.ds(r, tm // d, stride=d), :].astype(BF16)


def _a_qkv(x, gain, w_qkv, tabs, group, d):
    s = x.shape[0]
    tm, tn = QKV_TM, QKV_TN
    ncol = 3 * A_WIDTH // tn
    scratch = [pltpu.VMEM((tm, D_MODEL), BF16)]
    if d > 1:
        scratch.append(pltpu.VMEM((tn // LANES, tm, LANES), F32))
    tab_spec = pl.BlockSpec((tm, LANES), lambda i, j: (i, 0))
    return pl.pallas_call(
        functools.partial(_a_qkv_kernel, d=d, tm=tm, tn=tn),
        grid=(s // tm, ncol),
        in_specs=[
            pl.BlockSpec((tm, D_MODEL), lambda i, j: (i, 0)),
            pl.BlockSpec((1, D_MODEL), lambda i, j: (0, 0)),
            pl.BlockSpec((D_MODEL, tn), lambda i, j: (0, group * ncol + j)),
            tab_spec, tab_spec, tab_spec,
        ],
        out_specs=pl.BlockSpec((d, tm // d, tn), lambda i, j: (0, i, j)),
        out_shape=jax.ShapeDtypeStruct((d, s // d, 3 * A_WIDTH), BF16),
        scratch_shapes=scratch,
        compiler_params=_params("parallel", "arbitrary"),
        name=f"a_qkv_d{d}",
    )(x, gain, w_qkv, *tabs)


def _a_attn_kernel(q_ref, kp_ref, kc_ref, vp_ref, vc_ref, o_ref, lse_ref, *, tq):
    n = pl.program_id(1)
    blk = A_SPAN_BLK
    qi = lax.broadcasted_iota(jnp.int32, (blk, 2 * blk), 0)
    kj = lax.broadcasted_iota(jnp.int32, (blk, 2 * blk), 1)
    band = (kj >= qi) & (kj <= qi + blk)
    band_first = band & ((kj >= blk) | (n > 0))
    lane = lax.broadcasted_iota(jnp.int32, (blk, LANES), 1)
    ones = jnp.ones((2 * blk, LANES), BF16)
    nt = (((1,), (1,)), ((), ()))
    for b in range(tq // blk):
        rows = slice(b * blk, (b + 1) * blk)
        lse_tile = jnp.zeros((blk, LANES), F32)
        for h in range(A_HEADS):
            sl = slice(h * A_HEAD_DIM, (h + 1) * A_HEAD_DIM)
            if b == 0:
                k = jnp.concatenate([kp_ref[:, sl], kc_ref[0:blk, sl]], axis=0)
                v = jnp.concatenate([vp_ref[:, sl], vc_ref[0:blk, sl]], axis=0)
            else:
                k = kc_ref[(b - 1) * blk:(b + 1) * blk, sl]
                v = vc_ref[(b - 1) * blk:(b + 1) * blk, sl]
            s = lax.dot_general(q_ref[rows, sl], k, nt, preferred_element_type=F32)
            s = jnp.where(band_first if b == 0 else band, s, -jnp.inf)
            m = jnp.max(jnp.maximum(s[:, :blk], s[:, blk:]), axis=-1, keepdims=True)
            p = jnp.exp2(s - m).astype(BF16)
            oe = jnp.dot(p, jnp.concatenate([v, ones], axis=1), preferred_element_type=F32)
            l = oe[:, A_HEAD_DIM:]
            o_ref[rows, sl] = oe[:, :A_HEAD_DIM] / l
            lse = (m + jnp.log2(l)) * math.log(2.0)
            lse_tile = jnp.where(lane == h, lse, lse_tile)
        lse_ref[rows, :] = lse_tile


def _a_attn(qkv, d):
    _, m, _ = qkv.shape
    tq = A_ATT_TQ
    ratio = tq // A_SPAN_BLK
    cur = lambda col: pl.BlockSpec((None, tq, A_WIDTH), lambda r, n: (r, n, col))
    prev = lambda col: pl.BlockSpec(
        (None, A_SPAN_BLK, A_WIDTH), lambda r, n: (r, jnp.maximum(n * ratio - 1, 0), col))
    return pl.pallas_call(
        functools.partial(_a_attn_kernel, tq=tq),
        grid=(d, m // tq),
        in_specs=[cur(0), prev(1), cur(1), prev(2), cur(2)],
        out_specs=[
            pl.BlockSpec((None, tq, A_WIDTH), lambda r, n: (r, n, 0)),
            pl.BlockSpec((None, tq, LANES), lambda r, n: (r, n, 0)),
        ],
        out_shape=[
            jax.ShapeDtypeStruct((d, m, A_WIDTH), F32),
            jax.ShapeDtypeStruct((d, m, LANES), F32),
        ],
        compiler_params=_params("parallel", "arbitrary"),
        name=f"a_attn_d{d}",
    )(qkv, qkv, qkv, qkv, qkv)


def _a_wo_kernel(o1_ref, o2_ref, o3_ref, l1_ref, l2_ref, l3_ref, h_ref, w_ref, out_ref,
                 nat2_ref, nat3_ref, ln2_ref, ln3_ref, xb_ref, *, tm):
    for o_ref, l_ref, nat_ref, ln_ref, (_, d) in (
            (o2_ref, l2_ref, nat2_ref, ln2_ref, A_GROUPS[1]),
            (o3_ref, l3_ref, nat3_ref, ln3_ref, A_GROUPS[2])):
        for r in range(d):
            rows = pl.ds(r, tm // d, stride=d)
            ln_ref[0, rows, :] = l_ref[r]
            for h in range(A_HEADS):
                nat_ref[h, rows, :] = o_ref[r, :, h * A_HEAD_DIM:(h + 1) * A_HEAD_DIM]
    l1, l2, l3 = l1_ref[0], ln2_ref[0], ln3_ref[0]
    mx = jnp.maximum(jnp.maximum(l1, l2), l3)
    e1, e2, e3 = jnp.exp(l1 - mx), jnp.exp(l2 - mx), jnp.exp(l3 - mx)
    den = e1 + e2 + e3
    w1, w2, w3 = e1 / den, e2 / den, e3 / den
    for h in range(A_HEADS):
        sl = slice(h * A_HEAD_DIM, (h + 1) * A_HEAD_DIM)
        merged = (w1[:, h:h + 1] * o1_ref[0, :, sl] + w2[:, h:h + 1] * nat2_ref[h]
                  + w3[:, h:h + 1] * nat3_ref[h])
        xb_ref[:, sl] = merged.astype(BF16)
    out_ref[...] = h_ref[...] + jnp.dot(xb_ref[...], w_ref[...], preferred_element_type=F32)


def _a_wo(outs, lses, h, w_o):
    s = h.shape[0]
    tm = A_WO_TM
    o_spec = lambda d: pl.BlockSpec((d, tm // d, A_WIDTH), lambda i: (0, i, 0))
    l_spec = lambda d: pl.BlockSpec((d, tm // d, LANES), lambda i: (0, i, 0))
    ds_ = [d for _, d in A_GROUPS]
    return pl.pallas_call(
        functools.partial(_a_wo_kernel, tm=tm),
        grid=(s // tm,),
        in_specs=[o_spec(ds_[0]), o_spec(ds_[1]), o_spec(ds_[2]),
                  l_spec(ds_[0]), l_spec(ds_[1]), l_spec(ds_[2]),
                  pl.BlockSpec((tm, D_MODEL), lambda i: (i, 0)),
                  pl.BlockSpec((A_WIDTH, D_MODEL), lambda i: (0, 0))],
        out_specs=pl.BlockSpec((tm, D_MODEL), lambda i: (i, 0)),
        out_shape=jax.ShapeDtypeStruct((s, D_MODEL), F32),
        scratch_shapes=[
            pltpu.VMEM((A_HEADS, tm, LANES), F32),
            pltpu.VMEM((A_HEADS, tm, LANES), F32),
            pltpu.VMEM((1, tm, LANES), F32),
            pltpu.VMEM((1, tm, LANES), F32),
            pltpu.VMEM((tm, A_WIDTH), BF16),
        ],
        compiler_params=_params("parallel"),
        name="a_wo",
    )(*outs, *lses, h, w_o)


def _ffn_kernel(h_ref, g_ref, wg_ref, wu_ref, wd_ref, fg_ref, out_ref, xn_ref, *, final_norm):
    f = pl.program_id(1)

    @pl.when(f == 0)
    def _():
        xn_ref[...] = (_rms_scale(h_ref[...]) * g_ref[...]).astype(BF16)
        out_ref[...] = jnp.zeros(out_ref.shape, F32)

    xn = xn_ref[...]
    gate = jnp.dot(xn, wg_ref[...], preferred_element_type=F32)
    up = jnp.dot(xn, wu_ref[...], preferred_element_type=F32)
    act = (gate * jax.nn.sigmoid(gate) * up).astype(BF16)
    out_ref[...] += jnp.dot(act, wd_ref[...], preferred_element_type=F32)

    @pl.when(f == pl.num_programs(1) - 1)
    def _():
        y = h_ref[...] + out_ref[...]
        if final_norm:
            y = _rms_scale(y) * fg_ref[...]
        out_ref[...] = y


def _ffn(h, gain, w_gu, w_down, final_gain, final_norm):
    s = h.shape[0]
    tm, tf = FFN_TM, FFN_TF
    nf = FFN_HIDDEN // tf
    return pl.pallas_call(
        functools.partial(_ffn_kernel, final_norm=final_norm),
        grid=(s // tm, nf),
        in_specs=[
            pl.BlockSpec((tm, D_MODEL), lambda i, f: (i, 0)),
            pl.BlockSpec((1, D_MODEL), lambda i, f: (0, 0)),
            pl.BlockSpec((D_MODEL, tf), lambda i, f: (0, f)),
            pl.BlockSpec((D_MODEL, tf), lambda i, f: (0, nf + f)),
            pl.BlockSpec((tf, D_MODEL), lambda i, f: (f, 0)),
            pl.BlockSpec((1, D_MODEL), lambda i, f: (0, 0)),
        ],
        out_specs=pl.BlockSpec((tm, D_MODEL), lambda i, f: (i, 0)),
        out_shape=jax.ShapeDtypeStruct((s, D_MODEL), F32),
        scratch_shapes=[pltpu.VMEM((tm, D_MODEL), BF16)],
        compiler_params=_params("parallel", "arbitrary"),
        name="ffn_final" if final_norm else "ffn",
    )(h, gain, w_gu, w_gu, w_down, final_gain)


def _b_proj_kernel(h_ref, ga_ref, gk_ref, wqa_ref, wkva_ref, gqa_ref, gkva_ref, wqb_ref, wkvb_ref,
                   c_ref, slo_ref, shi_ref, q_ref, k_ref, vt_ref):
    y = _rms_scale(h_ref[...])
    xa = (y * ga_ref[...]).astype(BF16)
    xk = (y * gk_ref[...]).astype(BF16)
    c, s_lo, s_hi = c_ref[...], slo_ref[...], shi_ref[...]

    cq = jnp.dot(xa, wqa_ref[...], preferred_element_type=F32)
    cq = (_rms_scale(cq) * gqa_ref[...]).astype(BF16)
    q = jnp.dot(cq, wqb_ref[...], preferred_element_type=F32)
    qscale = (B_NOPE + B_ROPE) ** -0.5 * math.log2(math.e)
    for h in range(B_HEADS):
        lo = h * B_QK_PAD
        q_ref[:, lo:lo + B_NOPE] = (q[:, lo:lo + B_NOPE] * qscale).astype(BF16)
        pe = _rope_tile(q[:, lo + B_NOPE:lo + B_QK_PAD], c, s_lo, s_hi, B_ROPE // 2)
        q_ref[:, lo + B_NOPE:lo + B_QK_PAD] = (pe * qscale).astype(BF16)

    ckv = jnp.dot(xk, wkva_ref[...], preferred_element_type=F32)
    lat = (_rms_scale(ckv[:, :B_KV_LORA]) * gkva_ref[...]).astype(BF16)
    k_pe = _rope_tile(ckv[:, B_KV_LORA:], c, s_lo, s_hi, B_ROPE // 2).astype(BF16)
    kv = jnp.dot(lat, wkvb_ref[...], preferred_element_type=F32)
    for h in range(B_HEADS):
        lo = h * B_QK_PAD
        k_ref[:, lo:lo + B_NOPE] = kv[:, lo:lo + B_NOPE].astype(BF16)
        k_ref[:, lo + B_NOPE:lo + B_QK_PAD] = k_pe
        vt_ref[h * B_VE:h * B_VE + B_V, :] = kv[:, lo + B_NOPE:lo + B_QK_PAD].T.astype(BF16)
        vt_ref[h * B_VE + B_V:(h + 1) * B_VE, :] = jnp.ones((B_VE - B_V, h_ref.shape[0]), BF16)


def _b_proj(h, g_attn, g_kv, w_q_a, w_kv_a, g_qa, g_kva, w_q_b, w_kv_b, tabs):
    s = h.shape[0]
    tm = B_PROJ_TM
    full = lambda a: pl.BlockSpec(a.shape, lambda i: (0,) * a.ndim)
    row = lambda width: pl.BlockSpec((tm, width), lambda i: (i, 0))
    return pl.pallas_call(
        _b_proj_kernel,
        grid=(s // tm,),
        in_specs=[row(D_MODEL), full(g_attn), full(g_kv), full(w_q_a), full(w_kv_a), full(g_qa),
                  full(g_kva), full(w_q_b), full(w_kv_b), row(LANES), row(LANES), row(LANES)],
        out_specs=[row(B_HEADS * B_QK_PAD), row(B_HEADS * B_QK_PAD),
                   pl.BlockSpec((B_HEADS * B_VE, tm), lambda i: (0, i))],
        out_shape=[
            jax.ShapeDtypeStruct((s, B_HEADS * B_QK_PAD), BF16),
            jax.ShapeDtypeStruct((s, B_HEADS * B_QK_PAD), BF16),
            jax.ShapeDtypeStruct((B_HEADS * B_VE, s), BF16),
        ],
        compiler_params=_params("parallel"),
        name="b_proj",
    )(h, g_attn, g_kv, w_q_a, w_kv_a, g_qa, g_kva, w_q_b, w_kv_b, *tabs)


def _b_attn_kernel(qi_ref, ki_ref, q_ref, k_ref, vt_ref, o_ref, m_ref, acc_ref, *, t, cq, kb):
    p = pl.program_id(1)
    qi, ki = qi_ref[p], ki_ref[p]
    nchunk = t // cq

    @pl.when(ki == 0)
    def _():
        m_ref[...] = jnp.full(m_ref.shape, -jnp.inf, F32)
        acc_ref[...] = jnp.zeros(acc_ref.shape, F32)

    def step(diag):
        rows = lambda c: (c + 1) * cq if diag else t

        def scores(c):
            kr = rows(c)
            s = lax.dot_general(k_ref[0:kr, :], q_ref[c * cq:(c + 1) * cq, :],
                                (((1,), (1,)), ((), ())), preferred_element_type=F32)
            if diag:
                k_pos = lax.broadcasted_iota(jnp.int32, (kr, cq), 0)
                q_pos = c * cq + lax.broadcasted_iota(jnp.int32, (kr, cq), 1)
                s = jnp.where(k_pos <= q_pos, s, -jnp.inf)
            return s

        def update(c, s):
            m_prev = m_ref[c]
            m_new = jnp.maximum(m_prev, jnp.max(s, axis=0, keepdims=True))
            alpha = jnp.exp2(m_prev - m_new)
            pv = None
            for j in range(rows(c) // kb):
                pt = jnp.exp2(s[j * kb:(j + 1) * kb] - m_new)
                d = jnp.dot(vt_ref[:, j * kb:(j + 1) * kb], pt.astype(BF16),
                            preferred_element_type=F32)
                pv = d if pv is None else pv + d
            acc_ref[c] = alpha * acc_ref[c] + pv
            m_ref[c] = m_new

        s_next = scores(0)
        for c in range(nchunk):
            s_cur = s_next
            if c + 1 < nchunk:
                s_next = scores(c + 1)
            update(c, s_cur)

    @pl.when(ki == qi)
    def _():
        step(True)
        for c in range(nchunk):
            acc = acc_ref[c]
            o = acc[:B_V] / acc[B_V:B_V + 1]
            o_ref[c * cq:(c + 1) * cq, :] = o.T.astype(o_ref.dtype)

    @pl.when(ki < qi)
    def _():
        step(False)


def _b_attn(q, k, vt):
    s = q.shape[0]
    t, cq = B_ATT_T, B_ATT_CQ
    tq = tk = t
    pairs = [(a, b) for a in range(s // t) for b in range(a + 1)]
    qi = jnp.asarray(np.array([a for a, _ in pairs], np.int32))
    ki = jnp.asarray(np.array([b for _, b in pairs], np.int32))
    grid_spec = pltpu.PrefetchScalarGridSpec(
        num_scalar_prefetch=2,
        grid=(B_HEADS, len(pairs)),
        in_specs=[
            pl.BlockSpec((tq, B_QK_PAD), lambda h, p, qi, ki: (qi[p], h)),
            pl.BlockSpec((tk, B_QK_PAD), lambda h, p, qi, ki: (ki[p], h)),
            pl.BlockSpec((B_VE, tk), lambda h, p, qi, ki: (h, ki[p])),
        ],
        out_specs=pl.BlockSpec((tq, B_V), lambda h, p, qi, ki: (qi[p], h)),
        scratch_shapes=[
            pltpu.VMEM((t // cq, 1, cq), F32),
            pltpu.VMEM((t // cq, B_VE, cq), F32),
        ],
    )
    return pl.pallas_call(
        functools.partial(_b_attn_kernel, t=t, cq=cq, kb=B_ATT_KB),
        grid_spec=grid_spec,
        out_shape=jax.ShapeDtypeStruct((s, B_HEADS * B_V), BF16),
        compiler_params=_params("parallel", "arbitrary"),
        name="b_attn",
    )(qi, ki, q, k, vt)


def _b_wo_kernel(o_ref, h_ref, w_ref, out_ref):
    out_ref[...] = h_ref[...] + jnp.dot(o_ref[...], w_ref[...], preferred_element_type=F32)


def _b_wo(o, h, w_o):
    s = h.shape[0]
    tm = B_WO_TM
    return pl.pallas_call(
        _b_wo_kernel,
        grid=(s // tm,),
        in_specs=[pl.BlockSpec((tm, B_HEADS * B_V), lambda i: (i, 0)),
                  pl.BlockSpec((tm, D_MODEL), lambda i: (i, 0)),
                  pl.BlockSpec((B_HEADS * B_V, D_MODEL), lambda i: (0, 0))],
        out_specs=pl.BlockSpec((tm, D_MODEL), lambda i: (i, 0)),
        out_shape=jax.ShapeDtypeStruct((s, D_MODEL), F32),
        compiler_params=_params("parallel"),
        name="b_wo",
    )(o, h, w_o)


def _rope_tables(seq_len, dim, passthrough):
    inv_freq = ROPE_THETA ** (-jnp.arange(0, dim, 2, dtype=F32) / dim)
    ang = jnp.arange(seq_len, dtype=F32)[:, None] * inv_freq[None, :]
    cos, sin = jnp.cos(ang), jnp.sin(ang)
    half = dim // 2
    rest = LANES - dim
    tail = jnp.full((seq_len, rest), passthrough, F32)
    zeros = lambda n: jnp.zeros((seq_len, n), F32)
    c = jnp.concatenate([cos, cos, tail], axis=1)
    s_lo = jnp.concatenate([-sin, zeros(half + rest)], axis=1)
    s_hi = jnp.concatenate([zeros(half), sin, zeros(rest)], axis=1)
    return c, s_lo, s_hi


def kernel(x, attn_norm_g, ffn_norm_g, a_w_qkv, a_w_o, kv_norm_g, b_w_kv_a, b_kv_a_norm_g, b_w_kv_b,
           b_w_q_a, b_q_a_norm_g, b_w_q_b, b_w_o, ffn_w_gu, ffn_w_down, final_norm_g):
    assert x.shape[0] == 1 and x.shape[2] == D_MODEL
    s = x.shape[1]
    assert s % QKV_TM == 0 and s % (A_GROUPS[-1][1] * A_ATT_TQ) == 0 and s % B_ATT_T == 0
    h = x[0]
    row = lambda g: g.reshape(1, -1).astype(F32)
    tabs_a = _rope_tables(s, A_ROT_DIM, 1.0)
    tabs_b = _rope_tables(s, B_ROPE, 0.0)

    w_qkv = a_w_qkv[0].astype(BF16)
    outs, lses = [], []
    for group, (_, d) in enumerate(A_GROUPS):
        qkv = _a_qkv(h, row(attn_norm_g[0]), w_qkv, tabs_a, group, d)
        o, lse = _a_attn(qkv, d)
        outs.append(o)
        lses.append(lse)
    h = _a_wo(outs, lses, h, a_w_o[0].astype(BF16))
    h = _ffn(h, row(ffn_norm_g[0]), ffn_w_gu[0].astype(BF16), ffn_w_down[0].astype(BF16),
             row(final_norm_g), final_norm=False)

    w_q_b = b_w_q_b[0].reshape(B_Q_LORA, B_HEADS, B_NOPE + B_ROPE)
    w_q_b = jnp.pad(w_q_b, ((0, 0), (0, 0), (0, B_QK_PAD - B_NOPE - B_ROPE)))
    w_q_b = w_q_b.reshape(B_Q_LORA, B_HEADS * B_QK_PAD).astype(BF16)
    w_kv_a = jnp.pad(b_w_kv_a, ((0, 0), (0, LANES - B_ROPE))).astype(BF16)
    q, k, vt = _b_proj(h, row(attn_norm_g[1]), row(kv_norm_g), b_w_q_a[0].astype(BF16), w_kv_a,
                       row(b_q_a_norm_g[0]), row(b_kv_a_norm_g), w_q_b, b_w_kv_b.astype(BF16), tabs_b)
    o = _b_attn(q, k, vt)
    h = _b_wo(o, h, b_w_o[0].astype(BF16))
    h = _ffn(h, row(ffn_norm_g[1]), ffn_w_gu[1].astype(BF16), ffn_w_down[1].astype(BF16),
             row(final_norm_g), final_norm=True)
    return h[None]
```

```python
import functools
import math

import jax
import jax.numpy as jnp
import numpy as np
from jax import lax
from jax.experimental import pallas as pl
from jax.experimental.pallas import tpu as pltpu

F32 = jnp.float32
BF16 = jnp.bfloat16

D_MODEL = 2048
ROPE_THETA = 500000.0
NORM_EPS = 1e-6
LANES = 128

A_HEAD_DIM = 128
A_HEADS = D_MODEL // A_HEAD_DIM
A_ROT_DIM = A_HEAD_DIM // 4
A_GROUPS = ((128, 1), (512, 4), (2048, 16))
A_WIDTH = A_HEADS * A_HEAD_DIM
A_SPAN_BLK = 128

B_HEADS = D_MODEL // 128
B_NOPE = 128
B_ROPE = 64
B_V = 128
B_VE = B_V + 16
B_Q_LORA = 512
B_KV_LORA = 512
B_QK_PAD = 256

FFN_HIDDEN = 5632

VMEM_LIMIT = 56 * 1024 * 1024

QKV_TM, QKV_TN = 1024, 1024
QKV_XSLABS = 4
A_ATT_TQ = 512
A_WO_TM = 256
FFN_TM, FFN_TF = 512, 512
B_PROJ_TM = 256
B_ATT_T, B_ATT_CQ = 2048, 512
B_ATT_KB = 256
B_WO_TM = 512


def _params(*sem, flags=None):
    return pltpu.CompilerParams(dimension_semantics=sem, vmem_limit_bytes=VMEM_LIMIT, flags=flags)


def _rms_scale(x):
    return x * lax.rsqrt(jnp.mean(x * x, axis=-1, keepdims=True) + NORM_EPS)


def _rope_tile(a, c, s_lo, s_hi, half):
    return (a * c + pltpu.roll(a, LANES - half, 1) * s_lo + pltpu.roll(a, half, 1) * s_hi)


def _a_qkv_kernel(x_ref, g_ref, w_ref, c_ref, slo_ref, shi_ref, o_ref, xn_ref, acc_ref, *xs,
                  d, tm, tn, ncol, nsteps):
    g = pl.program_id(0)
    sub = tm // d

    @pl.when(g == 0)
    def _():
        acc_ref[...] = jnp.zeros(acc_ref.shape, F32)

    @pl.when((g % ncol == 0) & (g < nsteps - 1))
    def _():
        if d == 1:
            xn_ref[...] = (_rms_scale(x_ref[...]) * g_ref[...]).astype(BF16)
        else:
            x = x_ref[...]
            inv = lax.rsqrt(jnp.mean(x * x, axis=-1, keepdims=True) + NORM_EPS)
            nslab = xs[0].shape[0]
            for base in range(0, D_MODEL // LANES, nslab):
                for k in range(nslab):
                    cols = slice((base + k) * LANES, (base + k + 1) * LANES)
                    xs[0][k] = x_ref[:, cols] * inv * g_ref[:, cols]
                for k in range(nslab):
                    for r in range(d):
                        xn_ref[r * sub:(r + 1) * sub, (base + k) * LANES:(base + k + 1) * LANES] = (
                            xs[0][k, pl.ds(r, sub, stride=d), :].astype(BF16))

    t = (jnp.maximum(g - 1, 0) % ncol) // (A_WIDTH // tn)
    rot = t < 2
    qscale = jnp.where(t == 0, A_HEAD_DIM ** -0.5 * math.log2(math.e), 1.0).astype(F32)
    c = jnp.where(rot, c_ref[...], 1.0) * qscale
    s_lo = jnp.where(rot, slo_ref[...], 0.0) * qscale
    s_hi = jnp.where(rot, shi_ref[...], 0.0) * qscale
    for col in range(tn // LANES):
        sl = slice(col * LANES, (col + 1) * LANES)
        a = _rope_tile(acc_ref[:, sl], c, s_lo, s_hi, A_ROT_DIM // 2).astype(BF16)
        for r in range(d):
            o_ref[r, :, sl] = a[r * sub:(r + 1) * sub]

    acc_ref[...] = jnp.dot(xn_ref[...], w_ref[...], preferred_element_type=F32)


def _a_qkv(x, gain, w_qkv, tabs, group, d):
    s = x.shape[0]
    tm, tn = QKV_TM, QKV_TN
    ncol = 3 * A_WIDTH // tn
    nsteps = (s // tm) * ncol + 1
    cur = lambda g: jnp.minimum(g, nsteps - 2)
    prev = lambda g: jnp.maximum(g - 1, 0)
    tab_spec = pl.BlockSpec((tm, LANES), lambda g: (prev(g) // ncol, 0))
    tabs = [tb.reshape(s // tm, tm // d, d, LANES).swapaxes(1, 2).reshape(s, LANES) for tb in tabs]
    scratch = [pltpu.VMEM((tm, D_MODEL), BF16), pltpu.VMEM((tm, tn), F32)]
    if d > 1:
        scratch.append(pltpu.VMEM((QKV_XSLABS, tm, LANES), F32))
    return pl.pallas_call(
        functools.partial(_a_qkv_kernel, d=d, tm=tm, tn=tn, ncol=ncol, nsteps=nsteps),
        grid=(nsteps,),
        in_specs=[
            pl.BlockSpec((tm, D_MODEL), lambda g: (cur(g) // ncol, 0)),
            pl.BlockSpec((1, D_MODEL), lambda g: (0, 0)),
            pl.BlockSpec((D_MODEL, tn), lambda g: (0, group * ncol + cur(g) % ncol)),
            tab_spec, tab_spec, tab_spec,
        ],
        out_specs=pl.BlockSpec((d, tm // d, tn), lambda g: (0, prev(g) // ncol, prev(g) % ncol)),
        out_shape=jax.ShapeDtypeStruct((d, s // d, 3 * A_WIDTH), BF16),
        scratch_shapes=scratch,
        compiler_params=_params("arbitrary"),
        name=f"a_qkv_d{d}",
    )(x, gain, w_qkv, *tabs)


def _a_attn_kernel(q_ref, kp_ref, kc_ref, vp_ref, vc_ref, o_ref, lse_ref, *, tq):
    n = pl.program_id(1)
    blk = A_SPAN_BLK
    qi = lax.broadcasted_iota(jnp.int32, (blk, 2 * blk), 0)
    kj = lax.broadcasted_iota(jnp.int32, (blk, 2 * blk), 1)
    band = (kj >= qi) & (kj <= qi + blk)
    band_first = band & ((kj >= blk) | (n > 0))
    lane = lax.broadcasted_iota(jnp.int32, (blk, LANES), 1)
    ones = jnp.ones((2 * blk, LANES), BF16)
    nt = (((1,), (1,)), ((), ()))
    for b in range(tq // blk):
        rows = slice(b * blk, (b + 1) * blk)
        lse_tile = jnp.zeros((blk, LANES), F32)
        for h in range(A_HEADS):
            sl = slice(h * A_HEAD_DIM, (h + 1) * A_HEAD_DIM)
            if b == 0:
                k = jnp.concatenate([kp_ref[:, sl], kc_ref[0:blk, sl]], axis=0)
                v = jnp.concatenate([vp_ref[:, sl], vc_ref[0:blk, sl]], axis=0)
            else:
                k = kc_ref[(b - 1) * blk:(b + 1) * blk, sl]
                v = vc_ref[(b - 1) * blk:(b + 1) * blk, sl]
            s = lax.dot_general(q_ref[rows, sl], k, nt, preferred_element_type=F32)
            s = jnp.where(band_first if b == 0 else band, s, -jnp.inf)
            m = jnp.max(jnp.maximum(s[:, :blk], s[:, blk:]), axis=-1, keepdims=True)
            p = jnp.exp2(s - m).astype(BF16)
            oe = jnp.dot(p, jnp.concatenate([v, ones], axis=1), preferred_element_type=F32)
            l = oe[:, A_HEAD_DIM:]
            o_ref[rows, sl] = oe[:, :A_HEAD_DIM] / l
            lse = (m + jnp.log2(l)) * math.log(2.0)
            lse_tile = jnp.where(lane == h, lse, lse_tile)
        lse_ref[rows, :] = lse_tile


def _a_attn(qkv, d):
    _, m, _ = qkv.shape
    tq = A_ATT_TQ
    ratio = tq // A_SPAN_BLK
    cur = lambda col: pl.BlockSpec((None, tq, A_WIDTH), lambda r, n: (r, n, col))
    prev = lambda col: pl.BlockSpec(
        (None, A_SPAN_BLK, A_WIDTH), lambda r, n: (r, jnp.maximum(n * ratio - 1, 0), col))
    return pl.pallas_call(
        functools.partial(_a_attn_kernel, tq=tq),
        grid=(d, m // tq),
        in_specs=[cur(0), prev(1), cur(1), prev(2), cur(2)],
        out_specs=[
            pl.BlockSpec((None, tq, A_WIDTH), lambda r, n: (r, n, 0)),
            pl.BlockSpec((None, tq, LANES), lambda r, n: (r, n, 0)),
        ],
        out_shape=[
            jax.ShapeDtypeStruct((d, m, A_WIDTH), F32),
            jax.ShapeDtypeStruct((d, m, LANES), F32),
        ],
        compiler_params=_params("parallel", "arbitrary"),
        name=f"a_attn_d{d}",
    )(qkv, qkv, qkv, qkv, qkv)


def _a_wo_kernel(o1_ref, o2_ref, o3_ref, l1_ref, l2_ref, l3_ref, h_ref, w_ref, out_ref,
                 nat2_ref, nat3_ref, ln2_ref, ln3_ref, xb_ref, *, tm):
    for o_ref, l_ref, nat_ref, ln_ref, (_, d) in (
            (o2_ref, l2_ref, nat2_ref, ln2_ref, A_GROUPS[1]),
            (o3_ref, l3_ref, nat3_ref, ln3_ref, A_GROUPS[2])):
        for r in range(d):
            rows = pl.ds(r, tm // d, stride=d)
            ln_ref[0, rows, :] = l_ref[r]
            for h in range(A_HEADS):
                nat_ref[h, rows, :] = o_ref[r, :, h * A_HEAD_DIM:(h + 1) * A_HEAD_DIM]
    l1, l2, l3 = l1_ref[0], ln2_ref[0], ln3_ref[0]
    mx = jnp.maximum(jnp.maximum(l1, l2), l3)
    e1, e2, e3 = jnp.exp(l1 - mx), jnp.exp(l2 - mx), jnp.exp(l3 - mx)
    den = e1 + e2 + e3
    w1, w2, w3 = e1 / den, e2 / den, e3 / den
    for h in range(A_HEADS):
        sl = slice(h * A_HEAD_DIM, (h + 1) * A_HEAD_DIM)
        merged = (w1[:, h:h + 1] * o1_ref[0, :, sl] + w2[:, h:h + 1] * nat2_ref[h]
                  + w3[:, h:h + 1] * nat3_ref[h])
        xb_ref[:, sl] = merged.astype(BF16)
    out_ref[...] = h_ref[...] + jnp.dot(xb_ref[...], w_ref[...], preferred_element_type=F32)


def _a_wo(outs, lses, h, w_o):
    s = h.shape[0]
    tm = A_WO_TM
    o_spec = lambda d: pl.BlockSpec((d, tm // d, A_WIDTH), lambda i: (0, i, 0))
    l_spec = lambda d: pl.BlockSpec((d, tm // d, LANES), lambda i: (0, i, 0))
    ds_ = [d for _, d in A_GROUPS]
    return pl.pallas_call(
        functools.partial(_a_wo_kernel, tm=tm),
        grid=(s // tm,),
        in_specs=[o_spec(ds_[0]), o_spec(ds_[1]), o_spec(ds_[2]),
                  l_spec(ds_[0]), l_spec(ds_[1]), l_spec(ds_[2]),
                  pl.BlockSpec((tm, D_MODEL), lambda i: (i, 0)),
                  pl.BlockSpec((A_WIDTH, D_MODEL), lambda i: (0, 0))],
        out_specs=pl.BlockSpec((tm, D_MODEL), lambda i: (i, 0)),
        out_shape=jax.ShapeDtypeStruct((s, D_MODEL), F32),
        scratch_shapes=[
            pltpu.VMEM((A_HEADS, tm, LANES), F32),
            pltpu.VMEM((A_HEADS, tm, LANES), F32),
            pltpu.VMEM((1, tm, LANES), F32),
            pltpu.VMEM((1, tm, LANES), F32),
            pltpu.VMEM((tm, A_WIDTH), BF16),
        ],
        compiler_params=_params("parallel"),
        name="a_wo",
    )(*outs, *lses, h, w_o)


def _ffn_kernel(h_ref, g_ref, wg_ref, wu_ref, wd_ref, fg_ref, out_ref, xn_ref, *, final_norm):
    f = pl.program_id(1)

    @pl.when(f == 0)
    def _():
        xn_ref[...] = (_rms_scale(h_ref[...]) * g_ref[...]).astype(BF16)
        out_ref[...] = jnp.zeros(out_ref.shape, F32)

    xn = xn_ref[...]
    gate = jnp.dot(xn, wg_ref[...], preferred_element_type=F32)
    up = jnp.dot(xn, wu_ref[...], preferred_element_type=F32)
    act = (gate * jax.nn.sigmoid(gate) * up).astype(BF16)
    out_ref[...] += jnp.dot(act, wd_ref[...], preferred_element_type=F32)

    @pl.when(f == pl.num_programs(1) - 1)
    def _():
        y = h_ref[...] + out_ref[...]
        if final_norm:
            y = _rms_scale(y) * fg_ref[...]
        out_ref[...] = y


def _ffn(h, gain, w_gu, w_down, final_gain, final_norm):
    s = h.shape[0]
    tm, tf = FFN_TM, FFN_TF
    nf = FFN_HIDDEN // tf
    return pl.pallas_call(
        functools.partial(_ffn_kernel, final_norm=final_norm),
        grid=(s // tm, nf),
        in_specs=[
            pl.BlockSpec((tm, D_MODEL), lambda i, f: (i, 0)),
            pl.BlockSpec((1, D_MODEL), lambda i, f: (0, 0)),
            pl.BlockSpec((D_MODEL, tf), lambda i, f: (0, f)),
            pl.BlockSpec((D_MODEL, tf), lambda i, f: (0, nf + f)),
            pl.BlockSpec((tf, D_MODEL), lambda i, f: (f, 0)),
            pl.BlockSpec((1, D_MODEL), lambda i, f: (0, 0)),
        ],
        out_specs=pl.BlockSpec((tm, D_MODEL), lambda i, f: (i, 0)),
        out_shape=jax.ShapeDtypeStruct((s, D_MODEL), F32),
        scratch_shapes=[pltpu.VMEM((tm, D_MODEL), BF16)],
        compiler_params=_params("parallel", "arbitrary"),
        name="ffn_final" if final_norm else "ffn",
    )(h, gain, w_gu, w_gu, w_down, final_gain)


def _b_proj_kernel(h_ref, ga_ref, gk_ref, wqa_ref, wkva_ref, gqa_ref, gkva_ref, wqb_ref, wkvb_ref,
                   c_ref, slo_ref, shi_ref, q_ref, k_ref, vt_ref):
    y = _rms_scale(h_ref[...])
    xa = (y * ga_ref[...]).astype(BF16)
    xk = (y * gk_ref[...]).astype(BF16)
    c, s_lo, s_hi = c_ref[...], slo_ref[...], shi_ref[...]

    cq = jnp.dot(xa, wqa_ref[...], preferred_element_type=F32)
    cq = (_rms_scale(cq) * gqa_ref[...]).astype(BF16)
    q = jnp.dot(cq, wqb_ref[...], preferred_element_type=F32)
    qscale = (B_NOPE + B_ROPE) ** -0.5 * math.log2(math.e)
    for h in range(B_HEADS):
        lo = h * B_QK_PAD
        q_ref[:, lo:lo + B_NOPE] = (q[:, lo:lo + B_NOPE] * qscale).astype(BF16)
        pe = _rope_tile(q[:, lo + B_NOPE:lo + B_QK_PAD], c, s_lo, s_hi, B_ROPE // 2)
        q_ref[:, lo + B_NOPE:lo + B_QK_PAD] = (pe * qscale).astype(BF16)

    ckv = jnp.dot(xk, wkva_ref[...], preferred_element_type=F32)
    lat = (_rms_scale(ckv[:, :B_KV_LORA]) * gkva_ref[...]).astype(BF16)
    k_pe = _rope_tile(ckv[:, B_KV_LORA:], c, s_lo, s_hi, B_ROPE // 2).astype(BF16)
    kv = jnp.dot(lat, wkvb_ref[...], preferred_element_type=F32)
    for h in range(B_HEADS):
        lo = h * B_QK_PAD
        k_ref[:, lo:lo + B_NOPE] = kv[:, lo:lo + B_NOPE].astype(BF16)
        k_ref[:, lo + B_NOPE:lo + B_QK_PAD] = k_pe
        vt_ref[h * B_VE:h * B_VE + B_V, :] = kv[:, lo + B_NOPE:lo + B_QK_PAD].T.astype(BF16)
        vt_ref[h * B_VE + B_V:(h + 1) * B_VE, :] = jnp.ones((B_VE - B_V, h_ref.shape[0]), BF16)


def _b_proj(h, g_attn, g_kv, w_q_a, w_kv_a, g_qa, g_kva, w_q_b, w_kv_b, tabs):
    s = h.shape[0]
    tm = B_PROJ_TM
    full = lambda a: pl.BlockSpec(a.shape, lambda i: (0,) * a.ndim)
    row = lambda width: pl.BlockSpec((tm, width), lambda i: (i, 0))
    return pl.pallas_call(
        _b_proj_kernel,
        grid=(s // tm,),
        in_specs=[row(D_MODEL), full(g_attn), full(g_kv), full(w_q_a), full(w_kv_a), full(g_qa),
                  full(g_kva), full(w_q_b), full(w_kv_b), row(LANES), row(LANES), row(LANES)],
        out_specs=[row(B_HEADS * B_QK_PAD), row(B_HEADS * B_QK_PAD),
                   pl.BlockSpec((B_HEADS * B_VE, tm), lambda i: (0, i))],
        out_shape=[
            jax.ShapeDtypeStruct((s, B_HEADS * B_QK_PAD), BF16),
            jax.ShapeDtypeStruct((s, B_HEADS * B_QK_PAD), BF16),
            jax.ShapeDtypeStruct((B_HEADS * B_VE, s), BF16),
        ],
        compiler_params=_params("parallel"),
        name="b_proj",
    )(h, g_attn, g_kv, w_q_a, w_kv_a, g_qa, g_kva, w_q_b, w_kv_b, *tabs)


def _b_attn_kernel(qi_ref, ki_ref, q_ref, k_ref, vt_ref, o_ref, m_ref, acc_ref, *, t, cq, kb):
    p = pl.program_id(1)
    qi, ki = qi_ref[p], ki_ref[p]
    nchunk = t // cq

    @pl.when(ki == 0)
    def _():
        m_ref[...] = jnp.full(m_ref.shape, -jnp.inf, F32)
        acc_ref[...] = jnp.zeros(acc_ref.shape, F32)

    def step(diag):
        rows = lambda c: (c + 1) * cq if diag else t

        def scores(c):
            kr = rows(c)
            s = lax.dot_general(k_ref[0:kr, :], q_ref[c * cq:(c + 1) * cq, :],
                                (((1,), (1,)), ((), ())), preferred_element_type=F32)
            if diag:
                k_pos = lax.broadcasted_iota(jnp.int32, (kr, cq), 0)
                q_pos = c * cq + lax.broadcasted_iota(jnp.int32, (kr, cq), 1)
                s = jnp.where(k_pos <= q_pos, s, -jnp.inf)
            return s

        def update(c, s):
            m_prev = m_ref[c]
            m_new = jnp.maximum(m_prev, jnp.max(s, axis=0, keepdims=True))
            alpha = jnp.exp2(m_prev - m_new)
            pv = None
            for j in range(rows(c) // kb):
                pt = jnp.exp2(s[j * kb:(j + 1) * kb] - m_new)
                d = jnp.dot(vt_ref[:, j * kb:(j + 1) * kb], pt.astype(BF16),
                            preferred_element_type=F32)
                pv = d if pv is None else pv + d
            acc_ref[c] = alpha * acc_ref[c] + pv
            m_ref[c] = m_new

        s_next = scores(0)
        for c in range(nchunk):
            s_cur = s_next
            if c + 1 < nchunk:
                s_next = scores(c + 1)
            update(c, s_cur)

    @pl.when(ki == qi)
    def _():
        step(True)
        for c in range(nchunk):
            acc = acc_ref[c]
            o = acc[:B_V] / acc[B_V:B_V + 1]
            o_ref[c * cq:(c + 1) * cq, :] = o.T.astype(o_ref.dtype)

    @pl.when(ki < qi)
    def _():
        step(False)


def _b_attn(q, k, vt):
    s = q.shape[0]
    t, cq = B_ATT_T, B_ATT_CQ
    tq = tk = t
    pairs = [(a, b) for a in range(s // t) for b in range(a + 1)]
    qi = jnp.asarray(np.array([a for a, _ in pairs], np.int32))
    ki = jnp.asarray(np.array([b for _, b in pairs], np.int32))
    grid_spec = pltpu.PrefetchScalarGridSpec(
        num_scalar_prefetch=2,
        grid=(B_HEADS, len(pairs)),
        in_specs=[
            pl.BlockSpec((tq, B_QK_PAD), lambda h, p, qi, ki: (qi[p], h)),
            pl.BlockSpec((tk, B_QK_PAD), lambda h, p, qi, ki: (ki[p], h)),
            pl.BlockSpec((B_VE, tk), lambda h, p, qi, ki: (h, ki[p])),
        ],
        out_specs=pl.BlockSpec((tq, B_V), lambda h, p, qi, ki: (qi[p], h)),
        scratch_shapes=[
            pltpu.VMEM((t // cq, 1, cq), F32),
            pltpu.VMEM((t // cq, B_VE, cq), F32),
        ],
    )
    return pl.pallas_call(
        functools.partial(_b_attn_kernel, t=t, cq=cq, kb=B_ATT_KB),
        grid_spec=grid_spec,
        out_shape=jax.ShapeDtypeStruct((s, B_HEADS * B_V), BF16),
        compiler_params=_params("parallel", "arbitrary"),
        name="b_attn",
    )(qi, ki, q, k, vt)


def _b_wo_kernel(o_ref, h_ref, w_ref, out_ref):
    out_ref[...] = h_ref[...] + jnp.dot(o_ref[...], w_ref[...], preferred_element_type=F32)


def _b_wo(o, h, w_o):
    s = h.shape[0]
    tm = B_WO_TM
    return pl.pallas_call(
        _b_wo_kernel,
        grid=(s // tm,),
        in_specs=[pl.BlockSpec((tm, B_HEADS * B_V), lambda i: (i, 0)),
                  pl.BlockSpec((tm, D_MODEL), lambda i: (i, 0)),
                  pl.BlockSpec((B_HEADS * B_V, D_MODEL), lambda i: (0, 0))],
        out_specs=pl.BlockSpec((tm, D_MODEL), lambda i: (i, 0)),
        out_shape=jax.ShapeDtypeStruct((s, D_MODEL), F32),
        compiler_params=_params("parallel"),
        name="b_wo",
    )(o, h, w_o)


def _rope_tables(seq_len, dim, passthrough):
    inv_freq = ROPE_THETA ** (-jnp.arange(0, dim, 2, dtype=F32) / dim)
    ang = jnp.arange(seq_len, dtype=F32)[:, None] * inv_freq[None, :]
    cos, sin = jnp.cos(ang), jnp.sin(ang)
    half = dim // 2
    rest = LANES - dim
    tail = jnp.full((seq_len, rest), passthrough, F32)
    zeros = lambda n: jnp.zeros((seq_len, n), F32)
    c = jnp.concatenate([cos, cos, tail], axis=1)
    s_lo = jnp.concatenate([-sin, zeros(half + rest)], axis=1)
    s_hi = jnp.concatenate([zeros(half), sin, zeros(rest)], axis=1)
    return c, s_lo, s_hi


def kernel(x, attn_norm_g, ffn_norm_g, a_w_qkv, a_w_o, kv_norm_g, b_w_kv_a, b_kv_a_norm_g, b_w_kv_b,
           b_w_q_a, b_q_a_norm_g, b_w_q_b, b_w_o, ffn_w_gu, ffn_w_down, final_norm_g):
    assert x.shape[0] == 1 and x.shape[2] == D_MODEL
    s = x.shape[1]
    assert s % QKV_TM == 0 and s % (A_GROUPS[-1][1] * A_ATT_TQ) == 0 and s % B_ATT_T == 0
    h = x[0]
    row = lambda g: g.reshape(1, -1).astype(F32)
    tabs_a = _rope_tables(s, A_ROT_DIM, 1.0)
    tabs_b = _rope_tables(s, B_ROPE, 0.0)

    w_qkv = a_w_qkv[0].astype(BF16)
    outs, lses = [], []
    for group, (_, d) in enumerate(A_GROUPS):
        qkv = _a_qkv(h, row(attn_norm_g[0]), w_qkv, tabs_a, group, d)
        o, lse = _a_attn(qkv, d)
        outs.append(o)
        lses.append(lse)
    h = _a_wo(outs, lses, h, a_w_o[0].astype(BF16))
    h = _ffn(h, row(ffn_norm_g[0]), ffn_w_gu[0].astype(BF16), ffn_w_down[0].astype(BF16),
             row(final_norm_g), final_norm=False)

    w_q_b = b_w_q_b[0].reshape(B_Q_LORA, B_HEADS, B_NOPE + B_ROPE)
    w_q_b = jnp.pad(w_q_b, ((0, 0), (0, 0), (0, B_QK_PAD - B_NOPE - B_ROPE)))
    w_q_b = w_q_b.reshape(B_Q_LORA, B_HEADS * B_QK_PAD).astype(BF16)
    w_kv_a = jnp.pad(b_w_kv_a, ((0, 0), (0, LANES - B_ROPE))).astype(BF16)
    q, k, vt = _b_proj(h, row(attn_norm_g[1]), row(kv_norm_g), b_w_q_a[0].astype(BF16), w_kv_a,
                       row(b_q_a_norm_g[0]), row(b_kv_a_norm_g), w_q_b, b_w_kv_b.astype(BF16), tabs_b)
    o = _b_attn(q, k, vt)
    h = _b_wo(o, h, b_w_o[0].astype(BF16))
    h = _ffn(h, row(ffn_norm_g[1]), ffn_w_gu[1].astype(BF16), ffn_w_down[1].astype(BF16),
             row(final_norm_g), final_norm=True)
    return h[None]
```

```python
import functools
import math

import jax
import jax.numpy as jnp
import numpy as np
from jax import lax
from jax.experimental import pallas as pl
from jax.experimental.pallas import tpu as pltpu

F32 = jnp.float32
BF16 = jnp.bfloat16

D_MODEL = 2048
ROPE_THETA = 500000.0
NORM_EPS = 1e-6
LANES = 128

A_HEAD_DIM = 128
A_HEADS = D_MODEL // A_HEAD_DIM
A_ROT_DIM = A_HEAD_DIM // 4
A_GROUPS = ((128, 1), (512, 4), (2048, 16))
A_WIDTH = A_HEADS * A_HEAD_DIM
A_SPAN_BLK = 128

B_HEADS = D_MODEL // 128
B_NOPE = 128
B_ROPE = 64
B_V = 128
B_VE = B_V + 16
B_Q_LORA = 512
B_KV_LORA = 512
B_QK_PAD = 256

FFN_HIDDEN = 5632

VMEM_LIMIT = 56 * 1024 * 1024

QKV_TM, QKV_TN = 1024, 1024
QKV_XSLABS = 4
A_ATT_TQ = 512
A_WO_TM = 256
FFN_TM, FFN_TF = 512, 512
B_PROJ_TM = 256
B_ATT_T, B_ATT_CQ = 2048, 512
B_ATT_KB = 256
B_ATT_KP = 512
B_WO_TM = 512


def _params(*sem, flags=None):
    return pltpu.CompilerParams(dimension_semantics=sem, vmem_limit_bytes=VMEM_LIMIT, flags=flags)


def _rms_scale(x):
    return x * lax.rsqrt(jnp.mean(x * x, axis=-1, keepdims=True) + NORM_EPS)


def _rope_tile(a, c, s_lo, s_hi, half):
    return (a * c + pltpu.roll(a, LANES - half, 1) * s_lo + pltpu.roll(a, half, 1) * s_hi)


def _a_qkv_kernel(x_ref, g_ref, w_ref, c_ref, slo_ref, shi_ref, o_ref, xn_ref, acc_ref, *xs,
                  d, tm, tn, ncol, nsteps):
    g = pl.program_id(0)
    sub = tm // d

    @pl.when(g == 0)
    def _():
        acc_ref[...] = jnp.zeros(acc_ref.shape, F32)

    @pl.when((g % ncol == 0) & (g < nsteps - 1))
    def _():
        if d == 1:
            xn_ref[...] = (_rms_scale(x_ref[...]) * g_ref[...]).astype(BF16)
        else:
            x = x_ref[...]
            inv = lax.rsqrt(jnp.mean(x * x, axis=-1, keepdims=True) + NORM_EPS)
            nslab = xs[0].shape[0]
            for base in range(0, D_MODEL // LANES, nslab):
                for k in range(nslab):
                    cols = slice((base + k) * LANES, (base + k + 1) * LANES)
                    xs[0][k] = x_ref[:, cols] * inv * g_ref[:, cols]
                for k in range(nslab):
                    for r in range(d):
                        xn_ref[r * sub:(r + 1) * sub, (base + k) * LANES:(base + k + 1) * LANES] = (
                            xs[0][k, pl.ds(r, sub, stride=d), :].astype(BF16))

    t = (jnp.maximum(g - 1, 0) % ncol) // (A_WIDTH // tn)
    rot = t < 2
    qscale = jnp.where(t == 0, A_HEAD_DIM ** -0.5 * math.log2(math.e), 1.0).astype(F32)
    c = jnp.where(rot, c_ref[...], 1.0) * qscale
    s_lo = jnp.where(rot, slo_ref[...], 0.0) * qscale
    s_hi = jnp.where(rot, shi_ref[...], 0.0) * qscale
    for col in range(tn // LANES):
        sl = slice(col * LANES, (col + 1) * LANES)
        a = _rope_tile(acc_ref[:, sl], c, s_lo, s_hi, A_ROT_DIM // 2).astype(BF16)
        for r in range(d):
            o_ref[r, :, sl] = a[r * sub:(r + 1) * sub]

    acc_ref[...] = jnp.dot(xn_ref[...], w_ref[...], preferred_element_type=F32)


def _a_qkv(x, gain, w_qkv, tabs, group, d):
    s = x.shape[0]
    tm, tn = QKV_TM, QKV_TN
    ncol = 3 * A_WIDTH // tn
    nsteps = (s // tm) * ncol + 1
    cur = lambda g: jnp.minimum(g, nsteps - 2)
    prev = lambda g: jnp.maximum(g - 1, 0)
    tab_spec = pl.BlockSpec((tm, LANES), lambda g: (prev(g) // ncol, 0))
    tabs = [tb.reshape(s // tm, tm // d, d, LANES).swapaxes(1, 2).reshape(s, LANES) for tb in tabs]
    scratch = [pltpu.VMEM((tm, D_MODEL), BF16), pltpu.VMEM((tm, tn), F32)]
    if d > 1:
        scratch.append(pltpu.VMEM((QKV_XSLABS, tm, LANES), F32))
    return pl.pallas_call(
        functools.partial(_a_qkv_kernel, d=d, tm=tm, tn=tn, ncol=ncol, nsteps=nsteps),
        grid=(nsteps,),
        in_specs=[
            pl.BlockSpec((tm, D_MODEL), lambda g: (cur(g) // ncol, 0)),
            pl.BlockSpec((1, D_MODEL), lambda g: (0, 0)),
            pl.BlockSpec((D_MODEL, tn), lambda g: (0, group * ncol + cur(g) % ncol)),
            tab_spec, tab_spec, tab_spec,
        ],
        out_specs=pl.BlockSpec((d, tm // d, tn), lambda g: (0, prev(g) // ncol, prev(g) % ncol)),
        out_shape=jax.ShapeDtypeStruct((d, s // d, 3 * A_WIDTH), BF16),
        scratch_shapes=scratch,
        compiler_params=_params("arbitrary"),
        name=f"a_qkv_d{d}",
    )(x, gain, w_qkv, *tabs)


def _a_attn_kernel(q_ref, kp_ref, kc_ref, vp_ref, vc_ref, o_ref, lse_ref, *, tq):
    n = pl.program_id(1)
    blk = A_SPAN_BLK
    qi = lax.broadcasted_iota(jnp.int32, (blk, 2 * blk), 0)
    kj = lax.broadcasted_iota(jnp.int32, (blk, 2 * blk), 1)
    band = (kj >= qi) & (kj <= qi + blk)
    band_first = band & ((kj >= blk) | (n > 0))
    lane = lax.broadcasted_iota(jnp.int32, (blk, LANES), 1)
    ones = jnp.ones((2 * blk, LANES), BF16)
    nt = (((1,), (1,)), ((), ()))
    for b in range(tq // blk):
        rows = slice(b * blk, (b + 1) * blk)
        lse_tile = jnp.zeros((blk, LANES), F32)
        for h in range(A_HEADS):
            sl = slice(h * A_HEAD_DIM, (h + 1) * A_HEAD_DIM)
            if b == 0:
                k = jnp.concatenate([kp_ref[:, sl], kc_ref[0:blk, sl]], axis=0)
                v = jnp.concatenate([vp_ref[:, sl], vc_ref[0:blk, sl]], axis=0)
            else:
                k = kc_ref[(b - 1) * blk:(b + 1) * blk, sl]
                v = vc_ref[(b - 1) * blk:(b + 1) * blk, sl]
            s = lax.dot_general(q_ref[rows, sl], k, nt, preferred_element_type=F32)
            s = jnp.where(band_first if b == 0 else band, s, -jnp.inf)
            m = jnp.max(jnp.maximum(s[:, :blk], s[:, blk:]), axis=-1, keepdims=True)
            p = jnp.exp2(s - m).astype(BF16)
            oe = jnp.dot(p, jnp.concatenate([v, ones], axis=1), preferred_element_type=F32)
            l = oe[:, A_HEAD_DIM:]
            o_ref[rows, sl] = oe[:, :A_HEAD_DIM] / l
            lse = (m + jnp.log2(l)) * math.log(2.0)
            lse_tile = jnp.where(lane == h, lse, lse_tile)
        lse_ref[rows, :] = lse_tile


def _a_attn(qkv, d):
    _, m, _ = qkv.shape
    tq = A_ATT_TQ
    ratio = tq // A_SPAN_BLK
    cur = lambda col: pl.BlockSpec((None, tq, A_WIDTH), lambda r, n: (r, n, col))
    prev = lambda col: pl.BlockSpec(
        (None, A_SPAN_BLK, A_WIDTH), lambda r, n: (r, jnp.maximum(n * ratio - 1, 0), col))
    return pl.pallas_call(
        functools.partial(_a_attn_kernel, tq=tq),
        grid=(d, m // tq),
        in_specs=[cur(0), prev(1), cur(1), prev(2), cur(2)],
        out_specs=[
            pl.BlockSpec((None, tq, A_WIDTH), lambda r, n: (r, n, 0)),
            pl.BlockSpec((None, tq, LANES), lambda r, n: (r, n, 0)),
        ],
        out_shape=[
            jax.ShapeDtypeStruct((d, m, A_WIDTH), F32),
            jax.ShapeDtypeStruct((d, m, LANES), F32),
        ],
        compiler_params=_params("parallel", "arbitrary"),
        name=f"a_attn_d{d}",
    )(qkv, qkv, qkv, qkv, qkv)


def _a_wo_kernel(o1_ref, o2_ref, o3_ref, l1_ref, l2_ref, l3_ref, h_ref, w_ref, out_ref,
                 nat2_ref, nat3_ref, ln2_ref, ln3_ref, xb_ref, *, tm):
    for o_ref, l_ref, nat_ref, ln_ref, (_, d) in (
            (o2_ref, l2_ref, nat2_ref, ln2_ref, A_GROUPS[1]),
            (o3_ref, l3_ref, nat3_ref, ln3_ref, A_GROUPS[2])):
        for r in range(d):
            rows = pl.ds(r, tm // d, stride=d)
            ln_ref[0, rows, :] = l_ref[r]
            for h in range(A_HEADS):
                nat_ref[h, rows, :] = o_ref[r, :, h * A_HEAD_DIM:(h + 1) * A_HEAD_DIM]
    l1, l2, l3 = l1_ref[0], ln2_ref[0], ln3_ref[0]
    mx = jnp.maximum(jnp.maximum(l1, l2), l3)
    e1, e2, e3 = jnp.exp(l1 - mx), jnp.exp(l2 - mx), jnp.exp(l3 - mx)
    den = e1 + e2 + e3
    w1, w2, w3 = e1 / den, e2 / den, e3 / den
    for h in range(A_HEADS):
        sl = slice(h * A_HEAD_DIM, (h + 1) * A_HEAD_DIM)
        merged = (w1[:, h:h + 1] * o1_ref[0, :, sl] + w2[:, h:h + 1] * nat2_ref[h]
                  + w3[:, h:h + 1] * nat3_ref[h])
        xb_ref[:, sl] = merged.astype(BF16)
    out_ref[...] = h_ref[...] + jnp.dot(xb_ref[...], w_ref[...], preferred_element_type=F32)


def _a_wo(outs, lses, h, w_o):
    s = h.shape[0]
    tm = A_WO_TM
    o_spec = lambda d: pl.BlockSpec((d, tm // d, A_WIDTH), lambda i: (0, i, 0))
    l_spec = lambda d: pl.BlockSpec((d, tm // d, LANES), lambda i: (0, i, 0))
    ds_ = [d for _, d in A_GROUPS]
    return pl.pallas_call(
        functools.partial(_a_wo_kernel, tm=tm),
        grid=(s // tm,),
        in_specs=[o_spec(ds_[0]), o_spec(ds_[1]), o_spec(ds_[2]),
                  l_spec(ds_[0]), l_spec(ds_[1]), l_spec(ds_[2]),
                  pl.BlockSpec((tm, D_MODEL), lambda i: (i, 0)),
                  pl.BlockSpec((A_WIDTH, D_MODEL), lambda i: (0, 0))],
        out_specs=pl.BlockSpec((tm, D_MODEL), lambda i: (i, 0)),
        out_shape=jax.ShapeDtypeStruct((s, D_MODEL), F32),
        scratch_shapes=[
            pltpu.VMEM((A_HEADS, tm, LANES), F32),
            pltpu.VMEM((A_HEADS, tm, LANES), F32),
            pltpu.VMEM((1, tm, LANES), F32),
            pltpu.VMEM((1, tm, LANES), F32),
            pltpu.VMEM((tm, A_WIDTH), BF16),
        ],
        compiler_params=_params("parallel"),
        name="a_wo",
    )(*outs, *lses, h, w_o)


def _ffn_kernel(h_ref, g_ref, wg_ref, wu_ref, wd_ref, fg_ref, out_ref, xn_ref, *, final_norm):
    f = pl.program_id(1)

    @pl.when(f == 0)
    def _():
        xn_ref[...] = (_rms_scale(h_ref[...]) * g_ref[...]).astype(BF16)
        out_ref[...] = jnp.zeros(out_ref.shape, F32)

    xn = xn_ref[...]
    gate = jnp.dot(xn, wg_ref[...], preferred_element_type=F32)
    up = jnp.dot(xn, wu_ref[...], preferred_element_type=F32)
    act = (gate * jax.nn.sigmoid(gate) * up).astype(BF16)
    out_ref[...] += jnp.dot(act, wd_ref[...], preferred_element_type=F32)

    @pl.when(f == pl.num_programs(1) - 1)
    def _():
        y = h_ref[...] + out_ref[...]
        if final_norm:
            y = _rms_scale(y) * fg_ref[...]
        out_ref[...] = y


def _ffn(h, gain, w_gu, w_down, final_gain, final_norm):
    s = h.shape[0]
    tm, tf = FFN_TM, FFN_TF
    nf = FFN_HIDDEN // tf
    return pl.pallas_call(
        functools.partial(_ffn_kernel, final_norm=final_norm),
        grid=(s // tm, nf),
        in_specs=[
            pl.BlockSpec((tm, D_MODEL), lambda i, f: (i, 0)),
            pl.BlockSpec((1, D_MODEL), lambda i, f: (0, 0)),
            pl.BlockSpec((D_MODEL, tf), lambda i, f: (0, f)),
            pl.BlockSpec((D_MODEL, tf), lambda i, f: (0, nf + f)),
            pl.BlockSpec((tf, D_MODEL), lambda i, f: (f, 0)),
            pl.BlockSpec((1, D_MODEL), lambda i, f: (0, 0)),
        ],
        out_specs=pl.BlockSpec((tm, D_MODEL), lambda i, f: (i, 0)),
        out_shape=jax.ShapeDtypeStruct((s, D_MODEL), F32),
        scratch_shapes=[pltpu.VMEM((tm, D_MODEL), BF16)],
        compiler_params=_params("parallel", "arbitrary"),
        name="ffn_final" if final_norm else "ffn",
    )(h, gain, w_gu, w_gu, w_down, final_gain)


def _b_proj_kernel(h_ref, ga_ref, gk_ref, wqa_ref, wkva_ref, gqa_ref, gkva_ref, wqb_ref, wkvb_ref,
                   c_ref, slo_ref, shi_ref, q_ref, k_ref, vt_ref):
    y = _rms_scale(h_ref[...])
    xa = (y * ga_ref[...]).astype(BF16)
    xk = (y * gk_ref[...]).astype(BF16)
    c, s_lo, s_hi = c_ref[...], slo_ref[...], shi_ref[...]

    cq = jnp.dot(xa, wqa_ref[...], preferred_element_type=F32)
    cq = (_rms_scale(cq) * gqa_ref[...]).astype(BF16)
    q = jnp.dot(cq, wqb_ref[...], preferred_element_type=F32)
    qscale = (B_NOPE + B_ROPE) ** -0.5 * math.log2(math.e)
    for h in range(B_HEADS):
        lo = h * B_QK_PAD
        q_ref[:, lo:lo + B_NOPE] = (q[:, lo:lo + B_NOPE] * qscale).astype(BF16)
        pe = _rope_tile(q[:, lo + B_NOPE:lo + B_QK_PAD], c, s_lo, s_hi, B_ROPE // 2)
        q_ref[:, lo + B_NOPE:lo + B_QK_PAD] = (pe * qscale).astype(BF16)

    ckv = jnp.dot(xk, wkva_ref[...], preferred_element_type=F32)
    lat = (_rms_scale(ckv[:, :B_KV_LORA]) * gkva_ref[...]).astype(BF16)
    k_pe = _rope_tile(ckv[:, B_KV_LORA:], c, s_lo, s_hi, B_ROPE // 2).astype(BF16)
    kv = jnp.dot(lat, wkvb_ref[...], preferred_element_type=F32)
    for h in range(B_HEADS):
        lo = h * B_QK_PAD
        k_ref[:, lo:lo + B_NOPE] = kv[:, lo:lo + B_NOPE].astype(BF16)
        k_ref[:, lo + B_NOPE:lo + B_QK_PAD] = k_pe
        vt_ref[h * B_VE:h * B_VE + B_V, :] = kv[:, lo + B_NOPE:lo + B_QK_PAD].T.astype(BF16)
        vt_ref[h * B_VE + B_V:(h + 1) * B_VE, :] = jnp.ones((B_VE - B_V, h_ref.shape[0]), BF16)


def _b_proj(h, g_attn, g_kv, w_q_a, w_kv_a, g_qa, g_kva, w_q_b, w_kv_b, tabs):
    s = h.shape[0]
    tm = B_PROJ_TM
    full = lambda a: pl.BlockSpec(a.shape, lambda i: (0,) * a.ndim)
    row = lambda width: pl.BlockSpec((tm, width), lambda i: (i, 0))
    return pl.pallas_call(
        _b_proj_kernel,
        grid=(s // tm,),
        in_specs=[row(D_MODEL), full(g_attn), full(g_kv), full(w_q_a), full(w_kv_a), full(g_qa),
                  full(g_kva), full(w_q_b), full(w_kv_b), row(LANES), row(LANES), row(LANES)],
        out_specs=[row(B_HEADS * B_QK_PAD), row(B_HEADS * B_QK_PAD),
                   pl.BlockSpec((B_HEADS * B_VE, tm), lambda i: (0, i))],
        out_shape=[
            jax.ShapeDtypeStruct((s, B_HEADS * B_QK_PAD), BF16),
            jax.ShapeDtypeStruct((s, B_HEADS * B_QK_PAD), BF16),
            jax.ShapeDtypeStruct((B_HEADS * B_VE, s), BF16),
        ],
        compiler_params=_params("parallel"),
        name="b_proj",
    )(h, g_attn, g_kv, w_q_a, w_kv_a, g_qa, g_kva, w_q_b, w_kv_b, *tabs)


def _b_attn_kernel(qi_ref, ki_ref, hh_ref, q_ref, k_ref, vt_ref, o_ref, s_ref, m_ref,
                   alpha_ref, acc_ref, *, t, cq, kb, kp, nitems):
    g = pl.program_id(0)
    nchunk = t // cq
    a = jnp.minimum(g, nitems - 1)
    b = jnp.maximum(g - 1, 0)
    qi_a, ki_a = qi_ref[a], ki_ref[a]

    @pl.when(g == 0)
    def _():
        s_ref[...] = jnp.zeros(s_ref.shape, F32)
        m_ref[...] = jnp.zeros(m_ref.shape, F32)
        alpha_ref[...] = jnp.zeros(alpha_ref.shape, F32)
        acc_ref[...] = jnp.zeros(acc_ref.shape, F32)

    def body(diag_a, diag_b):
        for c in range(nchunk):
            m_b = m_ref[c]
            rows_a = (c + 1) * cq if diag_a else t
            rows_b = (c + 1) * cq if diag_b else t
            cmax, pv = None, None
            for i in range(t // kp):
                for j in range(i * kp // kb, min((i + 1) * kp, rows_b) // kb):
                    pt = jnp.exp2(s_ref[c, j * kb:(j + 1) * kb, :] - m_b)
                    d = jnp.dot(vt_ref[:, j * kb:(j + 1) * kb], pt.astype(BF16),
                                preferred_element_type=F32)
                    pv = d if pv is None else pv + d
                if i * kp < rows_a:
                    s = lax.dot_general(k_ref[i * kp:(i + 1) * kp, :],
                                        q_ref[c * cq:(c + 1) * cq, :],
                                        (((1,), (1,)), ((), ())), preferred_element_type=F32)
                    if diag_a and (i + 1) * kp > c * cq:
                        k_pos = i * kp + lax.broadcasted_iota(jnp.int32, (kp, cq), 0)
                        q_pos = c * cq + lax.broadcasted_iota(jnp.int32, (kp, cq), 1)
                        s = jnp.where(k_pos <= q_pos, s, -jnp.inf)
                    s_ref[c, i * kp:(i + 1) * kp, :] = s
                    pmax = jnp.max(s, axis=0, keepdims=True)
                    cmax = pmax if cmax is None else jnp.maximum(cmax, pmax)
            acc_ref[c] = alpha_ref[c] * acc_ref[c] + pv
            m_prev = jnp.where(ki_a == 0, -jnp.inf, m_b)
            m_new = jnp.maximum(m_prev, cmax)
            m_ref[c] = m_new
            alpha_ref[c] = jnp.exp2(m_prev - m_new)

    is_diag_b = ki_ref[b] == qi_ref[b]
    for diag_a in (False, True):
        for diag_b in (False, True):
            @pl.when(((ki_a == qi_a) == diag_a) & (is_diag_b == diag_b))
            def _(diag_a=diag_a, diag_b=diag_b):
                body(diag_a, diag_b)

    @pl.when(is_diag_b)
    def _():
        for c in range(nchunk):
            acc = acc_ref[c]
            o = acc[:B_V] / acc[B_V:B_V + 1]
            o_ref[c * cq:(c + 1) * cq, :] = o.T.astype(o_ref.dtype)


def _b_attn(q, k, vt):
    s = q.shape[0]
    t, cq = B_ATT_T, B_ATT_CQ
    items = [(h, a, b) for h in range(B_HEADS) for a in range(s // t) for b in range(a + 1)]
    n = len(items)
    hh, qi, ki = (jnp.asarray(np.array(col, np.int32)) for col in zip(*items))
    cur = lambda g: jnp.minimum(g, n - 1)
    prev = lambda g: jnp.maximum(g - 1, 0)
    nchunk = t // cq
    grid_spec = pltpu.PrefetchScalarGridSpec(
        num_scalar_prefetch=3,
        grid=(n + 1,),
        in_specs=[
            pl.BlockSpec((t, B_QK_PAD), lambda g, qi, ki, hh: (qi[cur(g)], hh[cur(g)])),
            pl.BlockSpec((t, B_QK_PAD), lambda g, qi, ki, hh: (ki[cur(g)], hh[cur(g)])),
            pl.BlockSpec((B_VE, t), lambda g, qi, ki, hh: (hh[prev(g)], ki[prev(g)])),
        ],
        out_specs=pl.BlockSpec((t, B_V), lambda g, qi, ki, hh: (qi[prev(g)], hh[prev(g)])),
        scratch_shapes=[
            pltpu.VMEM((nchunk, t, cq), F32),
            pltpu.VMEM((nchunk, 1, cq), F32),
            pltpu.VMEM((nchunk, 1, cq), F32),
            pltpu.VMEM((nchunk, B_VE, cq), F32),
        ],
    )
    return pl.pallas_call(
        functools.partial(_b_attn_kernel, t=t, cq=cq, kb=B_ATT_KB, kp=B_ATT_KP, nitems=n),
        grid_spec=grid_spec,
        out_shape=jax.ShapeDtypeStruct((s, B_HEADS * B_V), BF16),
        compiler_params=_params("arbitrary"),
        name="b_attn",
    )(qi, ki, hh, q, k, vt)


def _b_wo_kernel(o_ref, h_ref, w_ref, out_ref):
    out_ref[...] = h_ref[...] + jnp.dot(o_ref[...], w_ref[...], preferred_element_type=F32)


def _b_wo(o, h, w_o):
    s = h.shape[0]
    tm = B_WO_TM
    return pl.pallas_call(
        _b_wo_kernel,
        grid=(s // tm,),
        in_specs=[pl.BlockSpec((tm, B_HEADS * B_V), lambda i: (i, 0)),
                  pl.BlockSpec((tm, D_MODEL), lambda i: (i, 0)),
                  pl.BlockSpec((B_HEADS * B_V, D_MODEL), lambda i: (0, 0))],
        out_specs=pl.BlockSpec((tm, D_MODEL), lambda i: (i, 0)),
        out_shape=jax.ShapeDtypeStruct((s, D_MODEL), F32),
        compiler_params=_params("parallel"),
        name="b_wo",
    )(o, h, w_o)


def _rope_tables(seq_len, dim, passthrough):
    inv_freq = ROPE_THETA ** (-jnp.arange(0, dim, 2, dtype=F32) / dim)
    ang = jnp.arange(seq_len, dtype=F32)[:, None] * inv_freq[None, :]
    cos, sin = jnp.cos(ang), jnp.sin(ang)
    half = dim // 2
    rest = LANES - dim
    tail = jnp.full((seq_len, rest), passthrough, F32)
    zeros = lambda n: jnp.zeros((seq_len, n), F32)
    c = jnp.concatenate([cos, cos, tail], axis=1)
    s_lo = jnp.concatenate([-sin, zeros(half + rest)], axis=1)
    s_hi = jnp.concatenate([zeros(half), sin, zeros(rest)], axis=1)
    return c, s_lo, s_hi


def kernel(x, attn_norm_g, ffn_norm_g, a_w_qkv, a_w_o, kv_norm_g, b_w_kv_a, b_kv_a_norm_g, b_w_kv_b,
           b_w_q_a, b_q_a_norm_g, b_w_q_b, b_w_o, ffn_w_gu, ffn_w_down, final_norm_g):
    assert x.shape[0] == 1 and x.shape[2] == D_MODEL
    s = x.shape[1]
    assert s % QKV_TM == 0 and s % (A_GROUPS[-1][1] * A_ATT_TQ) == 0 and s % B_ATT_T == 0
    h = x[0]
    row = lambda g: g.reshape(1, -1).astype(F32)
    tabs_a = lax.optimization_barrier(_rope_tables(s, A_ROT_DIM, 1.0))
    tabs_b = _rope_tables(s, B_ROPE, 0.0)

    w_qkv = a_w_qkv[0].astype(BF16)
    outs, lses = [], []
    for group, (_, d) in enumerate(A_GROUPS):
        qkv = _a_qkv(h, row(attn_norm_g[0]), w_qkv, tabs_a, group, d)
        o, lse = _a_attn(qkv, d)
        outs.append(o)
        lses.append(lse)
    h = _a_wo(outs, lses, h, a_w_o[0].astype(BF16))
    h = _ffn(h, row(ffn_norm_g[0]), ffn_w_gu[0].astype(BF16), ffn_w_down[0].astype(BF16),
             row(final_norm_g), final_norm=False)

    w_q_b = b_w_q_b[0].reshape(B_Q_LORA, B_HEADS, B_NOPE + B_ROPE)
    w_q_b = jnp.pad(w_q_b, ((0, 0), (0, 0), (0, B_QK_PAD - B_NOPE - B_ROPE)))
    w_q_b = w_q_b.reshape(B_Q_LORA, B_HEADS * B_QK_PAD).astype(BF16)
    w_kv_a = jnp.pad(b_w_kv_a, ((0, 0), (0, LANES - B_ROPE))).astype(BF16)
    q, k, vt = _b_proj(h, row(attn_norm_g[1]), row(kv_norm_g), b_w_q_a[0].astype(BF16), w_kv_a,
                       row(b_q_a_norm_g[0]), row(b_kv_a_norm_g), w_q_b, b_w_kv_b.astype(BF16), tabs_b)
    o = _b_attn(q, k, vt)
    h = _b_wo(o, h, b_w_o[0].astype(BF16))
    h = _ffn(h, row(ffn_norm_g[1]), ffn_w_gu[1].astype(BF16), ffn_w_down[1].astype(BF16),
             row(final_norm_g), final_norm=True)
    return h[None]
```

```python
import functools
import math

import jax
import jax.numpy as jnp
import numpy as np
from jax import lax
from jax.experimental import pallas as pl
from jax.experimental.pallas import tpu as pltpu

F32 = jnp.float32
BF16 = jnp.bfloat16

D_MODEL = 2048
ROPE_THETA = 500000.0
NORM_EPS = 1e-6
LANES = 128

A_HEAD_DIM = 128
A_HEADS = D_MODEL // A_HEAD_DIM
A_ROT_DIM = A_HEAD_DIM // 4
A_GROUPS = ((128, 1), (512, 4), (2048, 16))
A_WIDTH = A_HEADS * A_HEAD_DIM
A_SPAN_BLK = 128

B_HEADS = D_MODEL // 128
B_NOPE = 128
B_ROPE = 64
B_V = 128
B_VE = B_V + 16
B_Q_LORA = 512
B_KV_LORA = 512
B_QK_PAD = 256

FFN_HIDDEN = 5632

VMEM_LIMIT = 56 * 1024 * 1024

QKV_TM, QKV_TN = 1024, 1024
QKV_XSLABS = 4
A_ATT_TQ = 512
A_WO_TM = 256
FFN_TM, FFN_TF = 512, 512
B_PROJ_TM = 256
B_ATT_T, B_ATT_CQ = 2048, 512
B_ATT_KB = 256
B_ATT_KP = 512
B_WO_TM = 512


def _params(*sem, flags=None):
    return pltpu.CompilerParams(dimension_semantics=sem, vmem_limit_bytes=VMEM_LIMIT, flags=flags)


def _rms_scale(x):
    return x * lax.rsqrt(jnp.mean(x * x, axis=-1, keepdims=True) + NORM_EPS)


def _rope_tile(a, c, s_lo, s_hi, half):
    return (a * c + pltpu.roll(a, LANES - half, 1) * s_lo + pltpu.roll(a, half, 1) * s_hi)


def _a_qkv_kernel(x_ref, g_ref, w_ref, c_ref, slo_ref, shi_ref, o_ref, xn_ref, acc_ref, *xs,
                  d, tm, tn, ncol, nsteps):
    g = pl.program_id(0)
    sub = tm // d

    @pl.when(g == 0)
    def _():
        acc_ref[...] = jnp.zeros(acc_ref.shape, F32)

    @pl.when((g % ncol == 0) & (g < nsteps - 1))
    def _():
        if d == 1:
            xn_ref[...] = (_rms_scale(x_ref[...]) * g_ref[...]).astype(BF16)
        else:
            x = x_ref[...]
            inv = lax.rsqrt(jnp.mean(x * x, axis=-1, keepdims=True) + NORM_EPS)
            nslab = xs[0].shape[0]
            for base in range(0, D_MODEL // LANES, nslab):
                for k in range(nslab):
                    cols = slice((base + k) * LANES, (base + k + 1) * LANES)
                    xs[0][k] = x_ref[:, cols] * inv * g_ref[:, cols]
                for k in range(nslab):
                    for r in range(d):
                        xn_ref[r * sub:(r + 1) * sub, (base + k) * LANES:(base + k + 1) * LANES] = (
                            xs[0][k, pl.ds(r, sub, stride=d), :].astype(BF16))

    t = (jnp.maximum(g - 1, 0) % ncol) // (A_WIDTH // tn)
    rot = t < 2
    qscale = jnp.where(t == 0, A_HEAD_DIM ** -0.5 * math.log2(math.e), 1.0).astype(F32)
    c = jnp.where(rot, c_ref[...], 1.0) * qscale
    s_lo = jnp.where(rot, slo_ref[...], 0.0) * qscale
    s_hi = jnp.where(rot, shi_ref[...], 0.0) * qscale
    for col in range(tn // LANES):
        sl = slice(col * LANES, (col + 1) * LANES)
        a = _rope_tile(acc_ref[:, sl], c, s_lo, s_hi, A_ROT_DIM // 2).astype(BF16)
        for r in range(d):
            o_ref[r, :, sl] = a[r * sub:(r + 1) * sub]

    acc_ref[...] = jnp.dot(xn_ref[...], w_ref[...], preferred_element_type=F32)


def _a_qkv(x, gain, w_qkv, tabs, group, d):
    s = x.shape[0]
    tm, tn = QKV_TM, QKV_TN
    ncol = 3 * A_WIDTH // tn
    nsteps = (s // tm) * ncol + 1
    cur = lambda g: jnp.minimum(g, nsteps - 2)
    prev = lambda g: jnp.maximum(g - 1, 0)
    tab_spec = pl.BlockSpec((tm, LANES), lambda g: (prev(g) // ncol, 0))
    tabs = [tb.reshape(s // tm, tm // d, d, LANES).swapaxes(1, 2).reshape(s, LANES) for tb in tabs]
    scratch = [pltpu.VMEM((tm, D_MODEL), BF16), pltpu.VMEM((tm, tn), F32)]
    if d > 1:
        scratch.append(pltpu.VMEM((QKV_XSLABS, tm, LANES), F32))
    return pl.pallas_call(
        functools.partial(_a_qkv_kernel, d=d, tm=tm, tn=tn, ncol=ncol, nsteps=nsteps),
        grid=(nsteps,),
        in_specs=[
            pl.BlockSpec((tm, D_MODEL), lambda g: (cur(g) // ncol, 0)),
            pl.BlockSpec((1, D_MODEL), lambda g: (0, 0)),
            pl.BlockSpec((D_MODEL, tn), lambda g: (0, group * ncol + cur(g) % ncol)),
            tab_spec, tab_spec, tab_spec,
        ],
        out_specs=pl.BlockSpec((d, tm // d, tn), lambda g: (0, prev(g) // ncol, prev(g) % ncol)),
        out_shape=jax.ShapeDtypeStruct((d, s // d, 3 * A_WIDTH), BF16),
        scratch_shapes=scratch,
        compiler_params=_params("arbitrary"),
        name=f"a_qkv_d{d}",
    )(x, gain, w_qkv, *tabs)


def _a_attn_kernel(q_ref, kp_ref, kc_ref, vp_ref, vc_ref, o_ref, lse_ref, *, tq):
    n = pl.program_id(1)
    blk = A_SPAN_BLK
    qi = lax.broadcasted_iota(jnp.int32, (blk, 2 * blk), 0)
    kj = lax.broadcasted_iota(jnp.int32, (blk, 2 * blk), 1)
    band = (kj >= qi) & (kj <= qi + blk)
    band_first = band & ((kj >= blk) | (n > 0))
    lane = lax.broadcasted_iota(jnp.int32, (blk, LANES), 1)
    ones = jnp.ones((2 * blk, LANES), BF16)
    nt = (((1,), (1,)), ((), ()))
    for b in range(tq // blk):
        rows = slice(b * blk, (b + 1) * blk)
        lse_tile = jnp.zeros((blk, LANES), F32)
        for h in range(A_HEADS):
            sl = slice(h * A_HEAD_DIM, (h + 1) * A_HEAD_DIM)
            if b == 0:
                k = jnp.concatenate([kp_ref[:, sl], kc_ref[0:blk, sl]], axis=0)
                v = jnp.concatenate([vp_ref[:, sl], vc_ref[0:blk, sl]], axis=0)
            else:
                k = kc_ref[(b - 1) * blk:(b + 1) * blk, sl]
                v = vc_ref[(b - 1) * blk:(b + 1) * blk, sl]
            s = lax.dot_general(q_ref[rows, sl], k, nt, preferred_element_type=F32)
            s = jnp.where(band_first if b == 0 else band, s, -jnp.inf)
            m = jnp.max(jnp.maximum(s[:, :blk], s[:, blk:]), axis=-1, keepdims=True)
            p = jnp.exp2(s - m).astype(BF16)
            oe = jnp.dot(p, jnp.concatenate([v, ones], axis=1), preferred_element_type=F32)
            l = oe[:, A_HEAD_DIM:]
            o_ref[rows, sl] = oe[:, :A_HEAD_DIM] / l
            lse = (m + jnp.log2(l)) * math.log(2.0)
            lse_tile = jnp.where(lane == h, lse, lse_tile)
        lse_ref[rows, :] = lse_tile


def _a_attn(qkv, d):
    _, m, _ = qkv.shape
    tq = A_ATT_TQ
    ratio = tq // A_SPAN_BLK
    cur = lambda col: pl.BlockSpec((None, tq, A_WIDTH), lambda r, n: (r, n, col))
    prev = lambda col: pl.BlockSpec(
        (None, A_SPAN_BLK, A_WIDTH), lambda r, n: (r, jnp.maximum(n * ratio - 1, 0), col))
    return pl.pallas_call(
        functools.partial(_a_attn_kernel, tq=tq),
        grid=(d, m // tq),
        in_specs=[cur(0), prev(1), cur(1), prev(2), cur(2)],
        out_specs=[
            pl.BlockSpec((None, tq, A_WIDTH), lambda r, n: (r, n, 0)),
            pl.BlockSpec((None, tq, LANES), lambda r, n: (r, n, 0)),
        ],
        out_shape=[
            jax.ShapeDtypeStruct((d, m, A_WIDTH), F32),
            jax.ShapeDtypeStruct((d, m, LANES), F32),
        ],
        compiler_params=_params("parallel", "arbitrary"),
        name=f"a_attn_d{d}",
    )(qkv, qkv, qkv, qkv, qkv)


def _a_wo_kernel(o1_ref, o2_ref, o3_ref, l1_ref, l2_ref, l3_ref, h_ref, w_ref, out_ref,
                 nat2_ref, nat3_ref, ln2_ref, ln3_ref, xb_ref, *, tm):
    for o_ref, l_ref, nat_ref, ln_ref, (_, d) in (
            (o2_ref, l2_ref, nat2_ref, ln2_ref, A_GROUPS[1]),
            (o3_ref, l3_ref, nat3_ref, ln3_ref, A_GROUPS[2])):
        for r in range(d):
            rows = pl.ds(r, tm // d, stride=d)
            ln_ref[0, rows, :] = l_ref[r]
            for h in range(A_HEADS):
                nat_ref[h, rows, :] = o_ref[r, :, h * A_HEAD_DIM:(h + 1) * A_HEAD_DIM]
    l1, l2, l3 = l1_ref[0], ln2_ref[0], ln3_ref[0]
    mx = jnp.maximum(jnp.maximum(l1, l2), l3)
    e1, e2, e3 = jnp.exp(l1 - mx), jnp.exp(l2 - mx), jnp.exp(l3 - mx)
    den = e1 + e2 + e3
    w1, w2, w3 = e1 / den, e2 / den, e3 / den
    for h in range(A_HEADS):
        sl = slice(h * A_HEAD_DIM, (h + 1) * A_HEAD_DIM)
        merged = (w1[:, h:h + 1] * o1_ref[0, :, sl] + w2[:, h:h + 1] * nat2_ref[h]
                  + w3[:, h:h + 1] * nat3_ref[h])
        xb_ref[:, sl] = merged.astype(BF16)
    out_ref[...] = h_ref[...] + jnp.dot(xb_ref[...], w_ref[...], preferred_element_type=F32)


def _a_wo(outs, lses, h, w_o):
    s = h.shape[0]
    tm = A_WO_TM
    o_spec = lambda d: pl.BlockSpec((d, tm // d, A_WIDTH), lambda i: (0, i, 0))
    l_spec = lambda d: pl.BlockSpec((d, tm // d, LANES), lambda i: (0, i, 0))
    ds_ = [d for _, d in A_GROUPS]
    return pl.pallas_call(
        functools.partial(_a_wo_kernel, tm=tm),
        grid=(s // tm,),
        in_specs=[o_spec(ds_[0]), o_spec(ds_[1]), o_spec(ds_[2]),
                  l_spec(ds_[0]), l_spec(ds_[1]), l_spec(ds_[2]),
                  pl.BlockSpec((tm, D_MODEL), lambda i: (i, 0)),
                  pl.BlockSpec((A_WIDTH, D_MODEL), lambda i: (0, 0))],
        out_specs=pl.BlockSpec((tm, D_MODEL), lambda i: (i, 0)),
        out_shape=jax.ShapeDtypeStruct((s, D_MODEL), F32),
        scratch_shapes=[
            pltpu.VMEM((A_HEADS, tm, LANES), F32),
            pltpu.VMEM((A_HEADS, tm, LANES), F32),
            pltpu.VMEM((1, tm, LANES), F32),
            pltpu.VMEM((1, tm, LANES), F32),
            pltpu.VMEM((tm, A_WIDTH), BF16),
        ],
        compiler_params=_params("parallel"),
        name="a_wo",
    )(*outs, *lses, h, w_o)


def _ffn_kernel(h_ref, g_ref, wg_ref, wu_ref, wd_ref, fg_ref, out_ref, xn_ref, *, final_norm):
    f = pl.program_id(1)

    @pl.when(f == 0)
    def _():
        xn_ref[...] = (_rms_scale(h_ref[...]) * g_ref[...]).astype(BF16)
        out_ref[...] = jnp.zeros(out_ref.shape, F32)

    xn = xn_ref[...]
    gate = jnp.dot(xn, wg_ref[...], preferred_element_type=F32)
    up = jnp.dot(xn, wu_ref[...], preferred_element_type=F32)
    act = (gate * jax.nn.sigmoid(gate) * up).astype(BF16)
    out_ref[...] += jnp.dot(act, wd_ref[...], preferred_element_type=F32)

    @pl.when(f == pl.num_programs(1) - 1)
    def _():
        y = h_ref[...] + out_ref[...]
        if final_norm:
            y = _rms_scale(y) * fg_ref[...]
        out_ref[...] = y


def _ffn(h, gain, w_gu, w_down, final_gain, final_norm):
    s = h.shape[0]
    tm, tf = FFN_TM, FFN_TF
    nf = FFN_HIDDEN // tf
    return pl.pallas_call(
        functools.partial(_ffn_kernel, final_norm=final_norm),
        grid=(s // tm, nf),
        in_specs=[
            pl.BlockSpec((tm, D_MODEL), lambda i, f: (i, 0)),
            pl.BlockSpec((1, D_MODEL), lambda i, f: (0, 0)),
            pl.BlockSpec((D_MODEL, tf), lambda i, f: (0, f)),
            pl.BlockSpec((D_MODEL, tf), lambda i, f: (0, nf + f)),
            pl.BlockSpec((tf, D_MODEL), lambda i, f: (f, 0)),
            pl.BlockSpec((1, D_MODEL), lambda i, f: (0, 0)),
        ],
        out_specs=pl.BlockSpec((tm, D_MODEL), lambda i, f: (i, 0)),
        out_shape=jax.ShapeDtypeStruct((s, D_MODEL), F32),
        scratch_shapes=[pltpu.VMEM((tm, D_MODEL), BF16)],
        compiler_params=_params("parallel", "arbitrary"),
        name="ffn_final" if final_norm else "ffn",
    )(h, gain, w_gu, w_gu, w_down, final_gain)


def _b_proj_kernel(h_ref, ga_ref, gk_ref, wqa_ref, wkva_ref, gqa_ref, gkva_ref, wqb_ref, wkvb_ref,
                   c_ref, slo_ref, shi_ref, ct_ref, st_ref, qt_ref, k_ref, vt_ref):
    y = _rms_scale(h_ref[...])
    xa = (y * ga_ref[...]).astype(BF16)
    xk = (y * gk_ref[...]).astype(BF16)
    c, s_lo, s_hi = c_ref[...], slo_ref[...], shi_ref[...]

    cq = jnp.dot(xa, wqa_ref[...], preferred_element_type=F32)
    cq = (_rms_scale(cq) * gqa_ref[...]).astype(BF16)
    qt = lax.dot_general(wqb_ref[...], cq, (((1,), (1,)), ((), ())),
                         preferred_element_type=F32)
    qscale = (B_NOPE + B_ROPE) ** -0.5 * math.log2(math.e)
    cos_t, sin_t = ct_ref[...], st_ref[...]
    half = B_ROPE // 2
    for h in range(B_HEADS):
        lo = h * B_QK_PAD
        src = h * (B_NOPE + B_ROPE)
        qt_ref[lo:lo + B_NOPE, :] = (qt[src:src + B_NOPE] * qscale).astype(BF16)
        x1 = qt[src + B_NOPE:src + B_NOPE + half]
        x2 = qt[src + B_NOPE + half:src + B_NOPE + B_ROPE]
        qt_ref[lo + B_NOPE:lo + B_NOPE + half, :] = (
            (x1 * cos_t - x2 * sin_t) * qscale).astype(BF16)
        qt_ref[lo + B_NOPE + half:lo + B_NOPE + B_ROPE, :] = (
            (x2 * cos_t + x1 * sin_t) * qscale).astype(BF16)
        qt_ref[lo + B_NOPE + B_ROPE:lo + B_QK_PAD, :] = jnp.zeros(
            (B_QK_PAD - B_NOPE - B_ROPE, qt.shape[1]), BF16)

    ckv = jnp.dot(xk, wkva_ref[...], preferred_element_type=F32)
    lat = (_rms_scale(ckv[:, :B_KV_LORA]) * gkva_ref[...]).astype(BF16)
    k_pe = _rope_tile(ckv[:, B_KV_LORA:], c, s_lo, s_hi, B_ROPE // 2).astype(BF16)
    kv = jnp.dot(lat, wkvb_ref[...], preferred_element_type=F32)
    for h in range(B_HEADS):
        lo = h * B_QK_PAD
        k_ref[:, lo:lo + B_NOPE] = kv[:, lo:lo + B_NOPE].astype(BF16)
        k_ref[:, lo + B_NOPE:lo + B_QK_PAD] = k_pe
        vt_ref[h * B_VE:h * B_VE + B_V, :] = kv[:, lo + B_NOPE:lo + B_QK_PAD].T.astype(BF16)
        vt_ref[h * B_VE + B_V:(h + 1) * B_VE, :] = jnp.ones((B_VE - B_V, h_ref.shape[0]), BF16)


def _b_proj(h, g_attn, g_kv, w_q_a, w_kv_a, g_qa, g_kva, w_q_b_t, w_kv_b, tabs, tabs_t):
    s = h.shape[0]
    tm = B_PROJ_TM
    full = lambda a: pl.BlockSpec(a.shape, lambda i: (0,) * a.ndim)
    row = lambda width: pl.BlockSpec((tm, width), lambda i: (i, 0))
    col = lambda height: pl.BlockSpec((height, tm), lambda i: (0, i))
    return pl.pallas_call(
        _b_proj_kernel,
        grid=(s // tm,),
        in_specs=[row(D_MODEL), full(g_attn), full(g_kv), full(w_q_a), full(w_kv_a), full(g_qa),
                  full(g_kva), full(w_q_b_t), full(w_kv_b), row(LANES), row(LANES), row(LANES),
                  col(B_ROPE // 2), col(B_ROPE // 2)],
        out_specs=[col(B_HEADS * B_QK_PAD), row(B_HEADS * B_QK_PAD), col(B_HEADS * B_VE)],
        out_shape=[
            jax.ShapeDtypeStruct((B_HEADS * B_QK_PAD, s), BF16),
            jax.ShapeDtypeStruct((s, B_HEADS * B_QK_PAD), BF16),
            jax.ShapeDtypeStruct((B_HEADS * B_VE, s), BF16),
        ],
        compiler_params=_params("parallel"),
        name="b_proj",
    )(h, g_attn, g_kv, w_q_a, w_kv_a, g_qa, g_kva, w_q_b_t, w_kv_b, *tabs, *tabs_t)


def _b_attn_kernel(qi_ref, ki_ref, hh_ref, qt_ref, k_ref, vt_ref, o_ref, s_ref, m_ref,
                   alpha_ref, acc_ref, *, t, cq, kb, kp, nitems):
    g = pl.program_id(0)
    nchunk = t // cq
    a = jnp.minimum(g, nitems - 1)
    b = jnp.maximum(g - 1, 0)
    qi_a, ki_a = qi_ref[a], ki_ref[a]

    @pl.when(g == 0)
    def _():
        s_ref[...] = jnp.zeros(s_ref.shape, F32)
        m_ref[...] = jnp.zeros(m_ref.shape, F32)
        alpha_ref[...] = jnp.zeros(alpha_ref.shape, F32)
        acc_ref[...] = jnp.zeros(acc_ref.shape, F32)

    def body(diag_a, diag_b):
        for c in range(nchunk):
            m_b = m_ref[c]
            rows_a = (c + 1) * cq if diag_a else t
            rows_b = (c + 1) * cq if diag_b else t
            cmax, pv = None, None
            for i in range(t // kp):
                for j in range(i * kp // kb, min((i + 1) * kp, rows_b) // kb):
                    pt = jnp.exp2(s_ref[c, j * kb:(j + 1) * kb, :] - m_b)
                    d = jnp.dot(vt_ref[:, j * kb:(j + 1) * kb], pt.astype(BF16),
                                preferred_element_type=F32)
                    pv = d if pv is None else pv + d
                if i * kp < rows_a:
                    s = jnp.dot(k_ref[i * kp:(i + 1) * kp, :], qt_ref[:, c * cq:(c + 1) * cq],
                                preferred_element_type=F32)
                    if diag_a and (i + 1) * kp > c * cq:
                        k_pos = i * kp + lax.broadcasted_iota(jnp.int32, (kp, cq), 0)
                        q_pos = c * cq + lax.broadcasted_iota(jnp.int32, (kp, cq), 1)
                        s = jnp.where(k_pos <= q_pos, s, -jnp.inf)
                    s_ref[c, i * kp:(i + 1) * kp, :] = s
                    pmax = jnp.max(s, axis=0, keepdims=True)
                    cmax = pmax if cmax is None else jnp.maximum(cmax, pmax)
            acc_ref[c] = alpha_ref[c] * acc_ref[c] + pv
            m_prev = jnp.where(ki_a == 0, -jnp.inf, m_b)
            m_new = jnp.maximum(m_prev, cmax)
            m_ref[c] = m_new
            alpha_ref[c] = jnp.exp2(m_prev - m_new)

    is_diag_b = ki_ref[b] == qi_ref[b]
    for diag_a in (False, True):
        for diag_b in (False, True):
            @pl.when(((ki_a == qi_a) == diag_a) & (is_diag_b == diag_b))
            def _(diag_a=diag_a, diag_b=diag_b):
                body(diag_a, diag_b)

    @pl.when(is_diag_b)
    def _():
        for c in range(nchunk):
            acc = acc_ref[c]
            o = acc[:B_V] / acc[B_V:B_V + 1]
            o_ref[c * cq:(c + 1) * cq, :] = o.T.astype(o_ref.dtype)


def _b_attn(qt, k, vt):
    s = k.shape[0]
    t, cq = B_ATT_T, B_ATT_CQ
    items = [(h, a, b) for h in range(B_HEADS) for a in range(s // t) for b in range(a + 1)]
    n = len(items)
    hh, qi, ki = (jnp.asarray(np.array(col, np.int32)) for col in zip(*items))
    cur = lambda g: jnp.minimum(g, n - 1)
    prev = lambda g: jnp.maximum(g - 1, 0)
    nchunk = t // cq
    grid_spec = pltpu.PrefetchScalarGridSpec(
        num_scalar_prefetch=3,
        grid=(n + 1,),
        in_specs=[
            pl.BlockSpec((B_QK_PAD, t), lambda g, qi, ki, hh: (hh[cur(g)], qi[cur(g)])),
            pl.BlockSpec((t, B_QK_PAD), lambda g, qi, ki, hh: (ki[cur(g)], hh[cur(g)])),
            pl.BlockSpec((B_VE, t), lambda g, qi, ki, hh: (hh[prev(g)], ki[prev(g)])),
        ],
        out_specs=pl.BlockSpec((t, B_V), lambda g, qi, ki, hh: (qi[prev(g)], hh[prev(g)])),
        scratch_shapes=[
            pltpu.VMEM((nchunk, t, cq), F32),
            pltpu.VMEM((nchunk, 1, cq), F32),
            pltpu.VMEM((nchunk, 1, cq), F32),
            pltpu.VMEM((nchunk, B_VE, cq), F32),
        ],
    )
    return pl.pallas_call(
        functools.partial(_b_attn_kernel, t=t, cq=cq, kb=B_ATT_KB, kp=B_ATT_KP, nitems=n),
        grid_spec=grid_spec,
        out_shape=jax.ShapeDtypeStruct((s, B_HEADS * B_V), BF16),
        compiler_params=_params("arbitrary"),
        name="b_attn",
    )(qi, ki, hh, qt, k, vt)


def _b_wo_kernel(o_ref, h_ref, w_ref, out_ref):
    out_ref[...] = h_ref[...] + jnp.dot(o_ref[...], w_ref[...], preferred_element_type=F32)


def _b_wo(o, h, w_o):
    s = h.shape[0]
    tm = B_WO_TM
    return pl.pallas_call(
        _b_wo_kernel,
        grid=(s // tm,),
        in_specs=[pl.BlockSpec((tm, B_HEADS * B_V), lambda i: (i, 0)),
                  pl.BlockSpec((tm, D_MODEL), lambda i: (i, 0)),
                  pl.BlockSpec((B_HEADS * B_V, D_MODEL), lambda i: (0, 0))],
        out_specs=pl.BlockSpec((tm, D_MODEL), lambda i: (i, 0)),
        out_shape=jax.ShapeDtypeStruct((s, D_MODEL), F32),
        compiler_params=_params("parallel"),
        name="b_wo",
    )(o, h, w_o)


def _rope_cos_sin(seq_len, dim):
    inv_freq = ROPE_THETA ** (-jnp.arange(0, dim, 2, dtype=F32) / dim)
    ang = jnp.arange(seq_len, dtype=F32)[:, None] * inv_freq[None, :]
    return jnp.cos(ang), jnp.sin(ang)


def _rope_tables(cos, sin, passthrough):
    seq_len, half = cos.shape
    dim = 2 * half
    rest = LANES - dim
    tail = jnp.full((seq_len, rest), passthrough, F32)
    zeros = lambda n: jnp.zeros((seq_len, n), F32)
    c = jnp.concatenate([cos, cos, tail], axis=1)
    s_lo = jnp.concatenate([-sin, zeros(half + rest)], axis=1)
    s_hi = jnp.concatenate([zeros(half), sin, zeros(rest)], axis=1)
    return c, s_lo, s_hi


def kernel(x, attn_norm_g, ffn_norm_g, a_w_qkv, a_w_o, kv_norm_g, b_w_kv_a, b_kv_a_norm_g, b_w_kv_b,
           b_w_q_a, b_q_a_norm_g, b_w_q_b, b_w_o, ffn_w_gu, ffn_w_down, final_norm_g):
    assert x.shape[0] == 1 and x.shape[2] == D_MODEL
    s = x.shape[1]
    assert s % QKV_TM == 0 and s % (A_GROUPS[-1][1] * A_ATT_TQ) == 0 and s % B_ATT_T == 0
    h = x[0]
    row = lambda g: g.reshape(1, -1).astype(F32)
    tabs_a = lax.optimization_barrier(_rope_tables(*_rope_cos_sin(s, A_ROT_DIM), 1.0))
    cos_b, sin_b = _rope_cos_sin(s, B_ROPE)
    tabs_b = _rope_tables(cos_b, sin_b, 0.0)
    tabs_bt = (cos_b.T, sin_b.T)

    w_qkv = a_w_qkv[0].astype(BF16)
    outs, lses = [], []
    for group, (_, d) in enumerate(A_GROUPS):
        qkv = _a_qkv(h, row(attn_norm_g[0]), w_qkv, tabs_a, group, d)
        o, lse = _a_attn(qkv, d)
        outs.append(o)
        lses.append(lse)
    h = _a_wo(outs, lses, h, a_w_o[0].astype(BF16))
    h = _ffn(h, row(ffn_norm_g[0]), ffn_w_gu[0].astype(BF16), ffn_w_down[0].astype(BF16),
             row(final_norm_g), final_norm=False)

    w_kv_a = jnp.pad(b_w_kv_a, ((0, 0), (0, LANES - B_ROPE))).astype(BF16)
    qt, k, vt = _b_proj(h, row(attn_norm_g[1]), row(kv_norm_g), b_w_q_a[0].astype(BF16), w_kv_a,
                        row(b_q_a_norm_g[0]), row(b_kv_a_norm_g), b_w_q_b[0].T.astype(BF16),
                        b_w_kv_b.astype(BF16), tabs_b, tabs_bt)
    o = _b_attn(qt, k, vt)
    h = _b_wo(o, h, b_w_o[0].astype(BF16))
    h = _ffn(h, row(ffn_norm_g[1]), ffn_w_gu[1].astype(BF16), ffn_w_down[1].astype(BF16),
             row(final_norm_g), final_norm=True)
    return h[None]
```

```python
import functools
import math

import jax
import jax.numpy as jnp
import numpy as np
from jax import lax
from jax.experimental import pallas as pl
from jax.experimental.pallas import tpu as pltpu

F32 = jnp.float32
BF16 = jnp.bfloat16

D_MODEL = 2048
ROPE_THETA = 500000.0
NORM_EPS = 1e-6
LANES = 128

A_HEAD_DIM = 128
A_HEADS = D_MODEL // A_HEAD_DIM
A_ROT_DIM = A_HEAD_DIM // 4
A_GROUPS = ((128, 1), (512, 4), (2048, 16))
A_WIDTH = A_HEADS * A_HEAD_DIM
A_SPAN_BLK = 128

B_HEADS = D_MODEL // 128
B_NOPE = 128
B_ROPE = 64
B_V = 128
B_VE = B_V + 16
B_Q_LORA = 512
B_KV_LORA = 512
B_QK_PAD = 256

FFN_HIDDEN = 5632

VMEM_LIMIT = 56 * 1024 * 1024

QKV_TM, QKV_TN = 1024, 1024
QKV_XSLABS = 4
A_ATT_TQ = 512
A_WO_TM = 256
A_WO_NCOL = 4
FFN_TM, FFN_TF = 512, 512
B_PROJ_TM = 256
B_ATT_T, B_ATT_CQ = 2048, 512
B_ATT_KB = 256
B_ATT_KP = 512
B_WO_TM = 512


def _params(*sem, flags=None):
    return pltpu.CompilerParams(dimension_semantics=sem, vmem_limit_bytes=VMEM_LIMIT, flags=flags)


def _rms_scale(x):
    return x * lax.rsqrt(jnp.mean(x * x, axis=-1, keepdims=True) + NORM_EPS)


def _rope_tile(a, c, s_lo, s_hi, half):
    return (a * c + pltpu.roll(a, LANES - half, 1) * s_lo + pltpu.roll(a, half, 1) * s_hi)


def _a_qkv_kernel(x_ref, g_ref, w_ref, c_ref, slo_ref, shi_ref, o_ref, xn_ref, acc_ref, *xs,
                  d, tm, tn, ncol, nsteps):
    g = pl.program_id(0)
    sub = tm // d

    @pl.when(g == 0)
    def _():
        acc_ref[...] = jnp.zeros(acc_ref.shape, F32)

    @pl.when((g % ncol == 0) & (g < nsteps - 1))
    def _():
        if d == 1:
            xn_ref[...] = (_rms_scale(x_ref[...]) * g_ref[...]).astype(BF16)
        else:
            x = x_ref[...]
            inv = lax.rsqrt(jnp.mean(x * x, axis=-1, keepdims=True) + NORM_EPS)
            nslab = xs[0].shape[0]
            for base in range(0, D_MODEL // LANES, nslab):
                for k in range(nslab):
                    cols = slice((base + k) * LANES, (base + k + 1) * LANES)
                    xs[0][k] = x_ref[:, cols] * inv * g_ref[:, cols]
                for k in range(nslab):
                    for r in range(d):
                        xn_ref[r * sub:(r + 1) * sub, (base + k) * LANES:(base + k + 1) * LANES] = (
                            xs[0][k, pl.ds(r, sub, stride=d), :].astype(BF16))

    t = (jnp.maximum(g - 1, 0) % ncol) // (A_WIDTH // tn)
    rot = t < 2
    qscale = jnp.where(t == 0, A_HEAD_DIM ** -0.5 * math.log2(math.e), 1.0).astype(F32)
    c = jnp.where(rot, c_ref[...], 1.0) * qscale
    s_lo = jnp.where(rot, slo_ref[...], 0.0) * qscale
    s_hi = jnp.where(rot, shi_ref[...], 0.0) * qscale
    for col in range(tn // LANES):
        sl = slice(col * LANES, (col + 1) * LANES)
        a = _rope_tile(acc_ref[:, sl], c, s_lo, s_hi, A_ROT_DIM // 2).astype(BF16)
        for r in range(d):
            o_ref[r, :, sl] = a[r * sub:(r + 1) * sub]

    acc_ref[...] = jnp.dot(xn_ref[...], w_ref[...], preferred_element_type=F32)


def _a_qkv(x, gain, w_qkv, tabs, group, d):
    s = x.shape[0]
    tm, tn = QKV_TM, QKV_TN
    ncol = 3 * A_WIDTH // tn
    nsteps = (s // tm) * ncol + 1
    cur = lambda g: jnp.minimum(g, nsteps - 2)
    prev = lambda g: jnp.maximum(g - 1, 0)
    tab_spec = pl.BlockSpec((tm, LANES), lambda g: (prev(g) // ncol, 0))
    tabs = [tb.reshape(s // tm, tm // d, d, LANES).swapaxes(1, 2).reshape(s, LANES) for tb in tabs]
    scratch = [pltpu.VMEM((tm, D_MODEL), BF16), pltpu.VMEM((tm, tn), F32)]
    if d > 1:
        scratch.append(pltpu.VMEM((QKV_XSLABS, tm, LANES), F32))
    return pl.pallas_call(
        functools.partial(_a_qkv_kernel, d=d, tm=tm, tn=tn, ncol=ncol, nsteps=nsteps),
        grid=(nsteps,),
        in_specs=[
            pl.BlockSpec((tm, D_MODEL), lambda g: (cur(g) // ncol, 0)),
            pl.BlockSpec((1, D_MODEL), lambda g: (0, 0)),
            pl.BlockSpec((D_MODEL, tn), lambda g: (0, group * ncol + cur(g) % ncol)),
            tab_spec, tab_spec, tab_spec,
        ],
        out_specs=pl.BlockSpec((d, tm // d, tn), lambda g: (0, prev(g) // ncol, prev(g) % ncol)),
        out_shape=jax.ShapeDtypeStruct((d, s // d, 3 * A_WIDTH), BF16),
        scratch_shapes=scratch,
        compiler_params=_params("arbitrary"),
        name=f"a_qkv_d{d}",
    )(x, gain, w_qkv, *tabs)


def _a_attn_kernel(q_ref, kp_ref, kc_ref, vp_ref, vc_ref, o_ref, lse_ref, *, tq):
    n = pl.program_id(1)
    blk = A_SPAN_BLK
    qi = lax.broadcasted_iota(jnp.int32, (blk, 2 * blk), 0)
    kj = lax.broadcasted_iota(jnp.int32, (blk, 2 * blk), 1)
    band = (kj >= qi) & (kj <= qi + blk)
    band_first = band & ((kj >= blk) | (n > 0))
    lane = lax.broadcasted_iota(jnp.int32, (blk, LANES), 1)
    ones = jnp.ones((2 * blk, LANES), BF16)
    nt = (((1,), (1,)), ((), ()))
    for b in range(tq // blk):
        rows = slice(b * blk, (b + 1) * blk)
        lse_tile = jnp.zeros((blk, LANES), F32)
        for h in range(A_HEADS):
            sl = slice(h * A_HEAD_DIM, (h + 1) * A_HEAD_DIM)
            if b == 0:
                k = jnp.concatenate([kp_ref[:, sl], kc_ref[0:blk, sl]], axis=0)
                v = jnp.concatenate([vp_ref[:, sl], vc_ref[0:blk, sl]], axis=0)
            else:
                k = kc_ref[(b - 1) * blk:(b + 1) * blk, sl]
                v = vc_ref[(b - 1) * blk:(b + 1) * blk, sl]
            s = lax.dot_general(q_ref[rows, sl], k, nt, preferred_element_type=F32)
            s = jnp.where(band_first if b == 0 else band, s, -jnp.inf)
            m = jnp.max(jnp.maximum(s[:, :blk], s[:, blk:]), axis=-1, keepdims=True)
            p = jnp.exp2(s - m).astype(BF16)
            oe = jnp.dot(p, jnp.concatenate([v, ones], axis=1), preferred_element_type=F32)
            l = oe[:, A_HEAD_DIM:]
            o_ref[rows, sl] = oe[:, :A_HEAD_DIM] / l
            lse = (m + jnp.log2(l)) * math.log(2.0)
            lse_tile = jnp.where(lane == h, lse, lse_tile)
        lse_ref[rows, :] = lse_tile


def _a_attn(qkv, d):
    _, m, _ = qkv.shape
    tq = A_ATT_TQ
    ratio = tq // A_SPAN_BLK
    cur = lambda col: pl.BlockSpec((None, tq, A_WIDTH), lambda r, n: (r, n, col))
    prev = lambda col: pl.BlockSpec(
        (None, A_SPAN_BLK, A_WIDTH), lambda r, n: (r, jnp.maximum(n * ratio - 1, 0), col))
    return pl.pallas_call(
        functools.partial(_a_attn_kernel, tq=tq),
        grid=(d, m // tq),
        in_specs=[cur(0), prev(1), cur(1), prev(2), cur(2)],
        out_specs=[
            pl.BlockSpec((None, tq, A_WIDTH), lambda r, n: (r, n, 0)),
            pl.BlockSpec((None, tq, LANES), lambda r, n: (r, n, 0)),
        ],
        out_shape=[
            jax.ShapeDtypeStruct((d, m, A_WIDTH), F32),
            jax.ShapeDtypeStruct((d, m, LANES), F32),
        ],
        compiler_params=_params("parallel", "arbitrary"),
        name=f"a_attn_d{d}",
    )(qkv, qkv, qkv, qkv, qkv)


def _a_wo_kernel(o1_ref, o2_ref, o3_ref, l1_ref, l2_ref, l3_ref, h_ref, w_ref, out_ref,
                 nat2_ref, nat3_ref, ln2_ref, ln3_ref, xb0_ref, xb1_ref, *, tm, ntiles, ncol):
    g = pl.program_id(0)
    heads_per_piece = A_HEADS // ncol
    tn = D_MODEL // ncol

    @pl.when(g == 0)
    def _():
        xb1_ref[...] = jnp.zeros(xb1_ref.shape, BF16)

    def body(xb_new, xb_old):
        for o_ref, l_ref, nat_ref, ln_ref, (_, d) in (
                (o2_ref, l2_ref, nat2_ref, ln2_ref, A_GROUPS[1]),
                (o3_ref, l3_ref, nat3_ref, ln3_ref, A_GROUPS[2])):
            for r in range(d):
                rows = pl.ds(r, tm // d, stride=d)
                ln_ref[0, rows, :] = l_ref[r]
                for h in range(A_HEADS):
                    nat_ref[h, rows, :] = o_ref[r, :, h * A_HEAD_DIM:(h + 1) * A_HEAD_DIM]
        l1, l2, l3 = l1_ref[0], ln2_ref[0], ln3_ref[0]
        mx = jnp.maximum(jnp.maximum(l1, l2), l3)
        e1, e2, e3 = jnp.exp(l1 - mx), jnp.exp(l2 - mx), jnp.exp(l3 - mx)
        den = e1 + e2 + e3
        w2, w3 = e2 / den, e3 / den
        for piece in range(ncol):
            for h in range(piece * heads_per_piece, (piece + 1) * heads_per_piece):
                sl = slice(h * A_HEAD_DIM, (h + 1) * A_HEAD_DIM)
                o1 = o1_ref[0, :, sl]
                merged = (o1 + w2[:, h:h + 1] * (nat2_ref[h] - o1)
                          + w3[:, h:h + 1] * (nat3_ref[h] - o1))
                xb_new[:, sl] = merged.astype(BF16)
            cols = slice(piece * tn, (piece + 1) * tn)
            out_ref[:, cols] = h_ref[:, cols] + jnp.dot(xb_old[...], w_ref[:, cols],
                                                         preferred_element_type=F32)

    for parity, bufs in enumerate(((xb0_ref, xb1_ref), (xb1_ref, xb0_ref))):
        @pl.when(g % 2 == parity)
        def _(bufs=bufs):
            body(*bufs)


def _a_wo(outs, lses, h, w_o):
    s = h.shape[0]
    tm = A_WO_TM
    ntiles = s // tm
    cur = lambda g: jnp.minimum(g, ntiles - 1)
    prev = lambda g: jnp.maximum(g - 1, 0)
    o_spec = lambda d: pl.BlockSpec((d, tm // d, A_WIDTH), lambda g: (0, cur(g), 0))
    l_spec = lambda d: pl.BlockSpec((d, tm // d, LANES), lambda g: (0, cur(g), 0))
    ds_ = [d for _, d in A_GROUPS]
    return pl.pallas_call(
        functools.partial(_a_wo_kernel, tm=tm, ntiles=ntiles, ncol=A_WO_NCOL),
        grid=(ntiles + 1,),
        in_specs=[o_spec(ds_[0]), o_spec(ds_[1]), o_spec(ds_[2]),
                  l_spec(ds_[0]), l_spec(ds_[1]), l_spec(ds_[2]),
                  pl.BlockSpec((tm, D_MODEL), lambda g: (prev(g), 0)),
                  pl.BlockSpec((A_WIDTH, D_MODEL), lambda g: (0, 0))],
        out_specs=pl.BlockSpec((tm, D_MODEL), lambda g: (prev(g), 0)),
        out_shape=jax.ShapeDtypeStruct((s, D_MODEL), F32),
        scratch_shapes=[
            pltpu.VMEM((A_HEADS, tm, LANES), F32),
            pltpu.VMEM((A_HEADS, tm, LANES), F32),
            pltpu.VMEM((1, tm, LANES), F32),
            pltpu.VMEM((1, tm, LANES), F32),
            pltpu.VMEM((tm, A_WIDTH), BF16),
            pltpu.VMEM((tm, A_WIDTH), BF16),
        ],
        compiler_params=_params("arbitrary"),
        name="a_wo",
    )(*outs, *lses, h, w_o)


def _ffn_kernel(h_ref, g_ref, wg_ref, wu_ref, wd_ref, fg_ref, out_ref, xn_ref, *, final_norm):
    f = pl.program_id(1)

    @pl.when(f == 0)
    def _():
        xn_ref[...] = (_rms_scale(h_ref[...]) * g_ref[...]).astype(BF16)
        out_ref[...] = jnp.zeros(out_ref.shape, F32)

    xn = xn_ref[...]
    gate = jnp.dot(xn, wg_ref[...], preferred_element_type=F32)
    up = jnp.dot(xn, wu_ref[...], preferred_element_type=F32)
    act = (gate * jax.nn.sigmoid(gate) * up).astype(BF16)
    out_ref[...] += jnp.dot(act, wd_ref[...], preferred_element_type=F32)

    @pl.when(f == pl.num_programs(1) - 1)
    def _():
        y = h_ref[...] + out_ref[...]
        if final_norm:
            y = _rms_scale(y) * fg_ref[...]
        out_ref[...] = y


def _ffn(h, gain, w_gu, w_down, layer, final_gain, final_norm):
    s = h.shape[0]
    tm, tf = FFN_TM, FFN_TF
    nf = FFN_HIDDEN // tf
    return pl.pallas_call(
        functools.partial(_ffn_kernel, final_norm=final_norm),
        grid=(s // tm, nf),
        in_specs=[
            pl.BlockSpec((tm, D_MODEL), lambda i, f: (i, 0)),
            pl.BlockSpec((1, D_MODEL), lambda i, f: (0, 0)),
            pl.BlockSpec((None, D_MODEL, tf), lambda i, f: (layer, 0, f)),
            pl.BlockSpec((None, D_MODEL, tf), lambda i, f: (layer, 0, nf + f)),
            pl.BlockSpec((None, tf, D_MODEL), lambda i, f: (layer, f, 0)),
            pl.BlockSpec((1, D_MODEL), lambda i, f: (0, 0)),
        ],
        out_specs=pl.BlockSpec((tm, D_MODEL), lambda i, f: (i, 0)),
        out_shape=jax.ShapeDtypeStruct((s, D_MODEL), F32),
        scratch_shapes=[pltpu.VMEM((tm, D_MODEL), BF16)],
        compiler_params=_params("parallel", "arbitrary"),
        name="ffn_final" if final_norm else "ffn",
    )(h, gain, w_gu, w_gu, w_down, final_gain)


def _b_proj_kernel(h_ref, ga_ref, gk_ref, wqa_ref, wkva_ref, gqa_ref, gkva_ref, wqb_ref, wkvb_ref,
                   c_ref, slo_ref, shi_ref, ct_ref, st_ref, qt_ref, k_ref, vt_ref):
    y = _rms_scale(h_ref[...])
    xa = (y * ga_ref[...]).astype(BF16)
    xk = (y * gk_ref[...]).astype(BF16)
    c, s_lo, s_hi = c_ref[...], slo_ref[...], shi_ref[...]

    cq = jnp.dot(xa, wqa_ref[...], preferred_element_type=F32)
    cq = (_rms_scale(cq) * gqa_ref[...]).astype(BF16)
    qt = lax.dot_general(wqb_ref[...], cq, (((1,), (1,)), ((), ())),
                         preferred_element_type=F32)
    qscale = (B_NOPE + B_ROPE) ** -0.5 * math.log2(math.e)
    cos_t, sin_t = ct_ref[...], st_ref[...]
    half = B_ROPE // 2
    for h in range(B_HEADS):
        lo = h * B_QK_PAD
        src = h * (B_NOPE + B_ROPE)
        qt_ref[lo:lo + B_NOPE, :] = (qt[src:src + B_NOPE] * qscale).astype(BF16)
        x1 = qt[src + B_NOPE:src + B_NOPE + half]
        x2 = qt[src + B_NOPE + half:src + B_NOPE + B_ROPE]
        qt_ref[lo + B_NOPE:lo + B_NOPE + half, :] = (
            (x1 * cos_t - x2 * sin_t) * qscale).astype(BF16)
        qt_ref[lo + B_NOPE + half:lo + B_NOPE + B_ROPE, :] = (
            (x2 * cos_t + x1 * sin_t) * qscale).astype(BF16)
        qt_ref[lo + B_NOPE + B_ROPE:lo + B_QK_PAD, :] = jnp.zeros(
            (B_QK_PAD - B_NOPE - B_ROPE, qt.shape[1]), BF16)

    ckv = jnp.dot(xk, wkva_ref[...], preferred_element_type=F32)
    lat = (_rms_scale(ckv[:, :B_KV_LORA]) * gkva_ref[...]).astype(BF16)
    k_pe = _rope_tile(ckv[:, B_KV_LORA:], c, s_lo, s_hi, B_ROPE // 2).astype(BF16)
    kv = jnp.dot(lat, wkvb_ref[...], preferred_element_type=F32)
    for h in range(B_HEADS):
        lo = h * B_QK_PAD
        k_ref[:, lo:lo + B_NOPE] = kv[:, lo:lo + B_NOPE].astype(BF16)
        k_ref[:, lo + B_NOPE:lo + B_QK_PAD] = k_pe
        vt_ref[h * B_VE:h * B_VE + B_V, :] = kv[:, lo + B_NOPE:lo + B_QK_PAD].T.astype(BF16)
        vt_ref[h * B_VE + B_V:(h + 1) * B_VE, :] = jnp.ones((B_VE - B_V, h_ref.shape[0]), BF16)


def _b_proj(h, g_attn, g_kv, w_q_a, w_kv_a, g_qa, g_kva, w_q_b_t, w_kv_b, tabs, tabs_t):
    s = h.shape[0]
    tm = B_PROJ_TM
    full = lambda a: pl.BlockSpec(a.shape, lambda i: (0,) * a.ndim)
    row = lambda width: pl.BlockSpec((tm, width), lambda i: (i, 0))
    col = lambda height: pl.BlockSpec((height, tm), lambda i: (0, i))
    return pl.pallas_call(
        _b_proj_kernel,
        grid=(s // tm,),
        in_specs=[row(D_MODEL), full(g_attn), full(g_kv), full(w_q_a), full(w_kv_a), full(g_qa),
                  full(g_kva), full(w_q_b_t), full(w_kv_b), row(LANES), row(LANES), row(LANES),
                  col(B_ROPE // 2), col(B_ROPE // 2)],
        out_specs=[col(B_HEADS * B_QK_PAD), row(B_HEADS * B_QK_PAD), col(B_HEADS * B_VE)],
        out_shape=[
            jax.ShapeDtypeStruct((B_HEADS * B_QK_PAD, s), BF16),
            jax.ShapeDtypeStruct((s, B_HEADS * B_QK_PAD), BF16),
            jax.ShapeDtypeStruct((B_HEADS * B_VE, s), BF16),
        ],
        compiler_params=_params("parallel"),
        name="b_proj",
    )(h, g_attn, g_kv, w_q_a, w_kv_a, g_qa, g_kva, w_q_b_t, w_kv_b, *tabs, *tabs_t)


def _b_attn_kernel(qi_ref, ki_ref, hh_ref, qt_ref, k_ref, vt_ref, o_ref, s_ref, m_ref,
                   alpha_ref, acc_ref, *, t, cq, kb, kp, nitems):
    g = pl.program_id(0)
    nchunk = t // cq
    a = jnp.minimum(g, nitems - 1)
    b = jnp.maximum(g - 1, 0)
    qi_a, ki_a = qi_ref[a], ki_ref[a]

    @pl.when(g == 0)
    def _():
        s_ref[...] = jnp.zeros(s_ref.shape, F32)
        m_ref[...] = jnp.zeros(m_ref.shape, F32)
        alpha_ref[...] = jnp.zeros(alpha_ref.shape, F32)
        acc_ref[...] = jnp.zeros(acc_ref.shape, F32)

    def body(diag_a, diag_b):
        for c in range(nchunk):
            m_b = m_ref[c]
            rows_a = (c + 1) * cq if diag_a else t
            rows_b = (c + 1) * cq if diag_b else t
            cmax, pv = None, None
            for i in range(t // kp):
                for j in range(i * kp // kb, min((i + 1) * kp, rows_b) // kb):
                    pt = jnp.exp2(s_ref[c, j * kb:(j + 1) * kb, :] - m_b)
                    d = jnp.dot(vt_ref[:, j * kb:(j + 1) * kb], pt.astype(BF16),
                                preferred_element_type=F32)
                    pv = d if pv is None else pv + d
                if i * kp < rows_a:
                    s = jnp.dot(k_ref[i * kp:(i + 1) * kp, :], qt_ref[:, c * cq:(c + 1) * cq],
                                preferred_element_type=F32)
                    if diag_a and (i + 1) * kp > c * cq:
                        k_pos = i * kp + lax.broadcasted_iota(jnp.int32, (kp, cq), 0)
                        q_pos = c * cq + lax.broadcasted_iota(jnp.int32, (kp, cq), 1)
                        s = jnp.where(k_pos <= q_pos, s, -jnp.inf)
                    s_ref[c, i * kp:(i + 1) * kp, :] = s
                    pmax = jnp.max(s, axis=0, keepdims=True)
                    cmax = pmax if cmax is None else jnp.maximum(cmax, pmax)
            acc_ref[c] = alpha_ref[c] * acc_ref[c] + pv
            m_prev = jnp.where(ki_a == 0, -jnp.inf, m_b)
            m_new = jnp.maximum(m_prev, cmax)
            m_ref[c] = m_new
            alpha_ref[c] = jnp.exp2(m_prev - m_new)

    is_diag_b = ki_ref[b] == qi_ref[b]
    for diag_a in (False, True):
        for diag_b in (False, True):
            @pl.when(((ki_a == qi_a) == diag_a) & (is_diag_b == diag_b))
            def _(diag_a=diag_a, diag_b=diag_b):
                body(diag_a, diag_b)

    @pl.when(is_diag_b)
    def _():
        for c in range(nchunk):
            acc = acc_ref[c]
            o = acc[:B_V] / acc[B_V:B_V + 1]
            o_ref[c * cq:(c + 1) * cq, :] = o.T.astype(o_ref.dtype)


def _b_attn(qt, k, vt):
    s = k.shape[0]
    t, cq = B_ATT_T, B_ATT_CQ
    items = [(h, a, b) for h in range(B_HEADS) for a in range(s // t) for b in range(a + 1)]
    n = len(items)
    hh, qi, ki = (jnp.asarray(np.array(col, np.int32)) for col in zip(*items))
    cur = lambda g: jnp.minimum(g, n - 1)
    prev = lambda g: jnp.maximum(g - 1, 0)
    nchunk = t // cq
    grid_spec = pltpu.PrefetchScalarGridSpec(
        num_scalar_prefetch=3,
        grid=(n + 1,),
        in_specs=[
            pl.BlockSpec((B_QK_PAD, t), lambda g, qi, ki, hh: (hh[cur(g)], qi[cur(g)])),
            pl.BlockSpec((t, B_QK_PAD), lambda g, qi, ki, hh: (ki[cur(g)], hh[cur(g)])),
            pl.BlockSpec((B_VE, t), lambda g, qi, ki, hh: (hh[prev(g)], ki[prev(g)])),
        ],
        out_specs=pl.BlockSpec((t, B_V), lambda g, qi, ki, hh: (qi[prev(g)], hh[prev(g)])),
        scratch_shapes=[
            pltpu.VMEM((nchunk, t, cq), F32),
            pltpu.VMEM((nchunk, 1, cq), F32),
            pltpu.VMEM((nchunk, 1, cq), F32),
            pltpu.VMEM((nchunk, B_VE, cq), F32),
        ],
    )
    return pl.pallas_call(
        functools.partial(_b_attn_kernel, t=t, cq=cq, kb=B_ATT_KB, kp=B_ATT_KP, nitems=n),
        grid_spec=grid_spec,
        out_shape=jax.ShapeDtypeStruct((s, B_HEADS * B_V), BF16),
        compiler_params=_params("arbitrary"),
        name="b_attn",
    )(qi, ki, hh, qt, k, vt)


def _b_wo_kernel(o_ref, h_ref, w_ref, out_ref):
    out_ref[...] = h_ref[...] + jnp.dot(o_ref[...], w_ref[...], preferred_element_type=F32)


def _b_wo(o, h, w_o):
    s = h.shape[0]
    tm = B_WO_TM
    return pl.pallas_call(
        _b_wo_kernel,
        grid=(s // tm,),
        in_specs=[pl.BlockSpec((tm, B_HEADS * B_V), lambda i: (i, 0)),
                  pl.BlockSpec((tm, D_MODEL), lambda i: (i, 0)),
                  pl.BlockSpec((B_HEADS * B_V, D_MODEL), lambda i: (0, 0))],
        out_specs=pl.BlockSpec((tm, D_MODEL), lambda i: (i, 0)),
        out_shape=jax.ShapeDtypeStruct((s, D_MODEL), F32),
        compiler_params=_params("parallel"),
        name="b_wo",
    )(o, h, w_o)


def _rope_cos_sin(seq_len, dim):
    inv_freq = ROPE_THETA ** (-jnp.arange(0, dim, 2, dtype=F32) / dim)
    ang = jnp.arange(seq_len, dtype=F32)[:, None] * inv_freq[None, :]
    return jnp.cos(ang), jnp.sin(ang)


def _rope_tables(cos, sin, passthrough):
    seq_len, half = cos.shape
    dim = 2 * half
    rest = LANES - dim
    tail = jnp.full((seq_len, rest), passthrough, F32)
    zeros = lambda n: jnp.zeros((seq_len, n), F32)
    c = jnp.concatenate([cos, cos, tail], axis=1)
    s_lo = jnp.concatenate([-sin, zeros(half + rest)], axis=1)
    s_hi = jnp.concatenate([zeros(half), sin, zeros(rest)], axis=1)
    return c, s_lo, s_hi


def kernel(x, attn_norm_g, ffn_norm_g, a_w_qkv, a_w_o, kv_norm_g, b_w_kv_a, b_kv_a_norm_g, b_w_kv_b,
           b_w_q_a, b_q_a_norm_g, b_w_q_b, b_w_o, ffn_w_gu, ffn_w_down, final_norm_g):
    assert x.shape[0] == 1 and x.shape[2] == D_MODEL
    s = x.shape[1]
    assert s % QKV_TM == 0 and s % (A_GROUPS[-1][1] * A_ATT_TQ) == 0 and s % B_ATT_T == 0
    h = x[0]
    row = lambda g: g.reshape(1, -1).astype(F32)
    tabs_a = lax.optimization_barrier(_rope_tables(*_rope_cos_sin(s, A_ROT_DIM), 1.0))
    cos_b, sin_b = _rope_cos_sin(s, B_ROPE)
    tabs_b = _rope_tables(cos_b, sin_b, 0.0)
    tabs_bt = (cos_b.T, sin_b.T)

    w_qkv = a_w_qkv[0].astype(BF16)
    outs, lses = [], []
    for group, (_, d) in enumerate(A_GROUPS):
        qkv = _a_qkv(h, row(attn_norm_g[0]), w_qkv, tabs_a, group, d)
        o, lse = _a_attn(qkv, d)
        outs.append(o)
        lses.append(lse)
    h = _a_wo(outs, lses, h, a_w_o[0].astype(BF16))
    w_gu, w_down = ffn_w_gu.astype(BF16), ffn_w_down.astype(BF16)
    h = _ffn(h, row(ffn_norm_g[0]), w_gu, w_down, 0, row(final_norm_g), final_norm=False)

    w_kv_a = jnp.pad(b_w_kv_a, ((0, 0), (0, LANES - B_ROPE))).astype(BF16)
    qt, k, vt = _b_proj(h, row(attn_norm_g[1]), row(kv_norm_g), b_w_q_a[0].astype(BF16), w_kv_a,
                        row(b_q_a_norm_g[0]), row(b_kv_a_norm_g), b_w_q_b[0].T.astype(BF16),
                        b_w_kv_b.astype(BF16), tabs_b, tabs_bt)
    o = _b_attn(qt, k, vt)
    h = _b_wo(o, h, b_w_o[0].astype(BF16))
    h = _ffn(h, row(ffn_norm_g[1]), w_gu, w_down, 1, row(final_norm_g), final_norm=True)
    return h[None]
```

```python
import functools
import math

import jax
import jax.numpy as jnp
import numpy as np
from jax import lax
from jax.experimental import pallas as pl
from jax.experimental.pallas import tpu as pltpu

F32 = jnp.float32
BF16 = jnp.bfloat16

D_MODEL = 2048
ROPE_THETA = 500000.0
NORM_EPS = 1e-6
LANES = 128

A_HEAD_DIM = 128
A_HEADS = D_MODEL // A_HEAD_DIM
A_ROT_DIM = A_HEAD_DIM // 4
A_GROUPS = ((128, 1), (512, 4), (2048, 16))
A_WIDTH = A_HEADS * A_HEAD_DIM
A_SPAN_BLK = 128

B_HEADS = D_MODEL // 128
B_NOPE = 128
B_ROPE = 64
B_V = 128
B_VE = B_V + 16
B_Q_LORA = 512
B_KV_LORA = 512
B_QK_PAD = 256

FFN_HIDDEN = 5632

VMEM_LIMIT = 56 * 1024 * 1024

QKV_TM, QKV_TN = 1024, 1024
QKV_XSLABS = 4
A_ATT_TQ = 512
A_WO_TM = 256
A_WO_NCOL = 4
FFN_TM, FFN_TF = 512, 512
B_PROJ_TM = 256
B_ATT_T, B_ATT_CQ = 2048, 512
B_ATT_KB = 256
B_ATT_KP = 512
B_WO_TM = 512


def _params(*sem, flags=None):
    return pltpu.CompilerParams(dimension_semantics=sem, vmem_limit_bytes=VMEM_LIMIT, flags=flags)


def _rms_scale(x):
    return x * lax.rsqrt(jnp.mean(x * x, axis=-1, keepdims=True) + NORM_EPS)


def _split_sin(s, half):
    lane = lax.broadcasted_iota(jnp.int32, s.shape, 1)
    return jnp.where(lane < half, -s, 0.0), jnp.where(lane >= half, s, 0.0)


def _rope_tile(a, c, s_lo, s_hi, half):
    return (a * c + pltpu.roll(a, LANES - half, 1) * s_lo + pltpu.roll(a, half, 1) * s_hi)


def _a_qkv_kernel(x_ref, g_ref, w_ref, c_ref, s_ref, o_ref, xn_ref, acc_ref, *xs,
                  d, tm, tn, ncol, nsteps):
    g = pl.program_id(0)
    sub = tm // d

    @pl.when(g == 0)
    def _():
        acc_ref[...] = jnp.zeros(acc_ref.shape, F32)

    @pl.when((g % ncol == 0) & (g < nsteps - 1))
    def _():
        if d == 1:
            xn_ref[...] = (_rms_scale(x_ref[...]) * g_ref[...]).astype(BF16)
        else:
            x = x_ref[...]
            inv = lax.rsqrt(jnp.mean(x * x, axis=-1, keepdims=True) + NORM_EPS)
            nslab = xs[0].shape[0]
            for base in range(0, D_MODEL // LANES, nslab):
                for k in range(nslab):
                    cols = slice((base + k) * LANES, (base + k + 1) * LANES)
                    xs[0][k] = x_ref[:, cols] * inv * g_ref[:, cols]
                for k in range(nslab):
                    for r in range(d):
                        xn_ref[r * sub:(r + 1) * sub, (base + k) * LANES:(base + k + 1) * LANES] = (
                            xs[0][k, pl.ds(r, sub, stride=d), :].astype(BF16))

    t = (jnp.maximum(g - 1, 0) % ncol) // (A_WIDTH // tn)
    rot = t < 2
    qscale = jnp.where(t == 0, A_HEAD_DIM ** -0.5 * math.log2(math.e), 1.0).astype(F32)
    c = jnp.where(rot, c_ref[...], 1.0) * qscale
    s_lo, s_hi = _split_sin(jnp.where(rot, s_ref[...], 0.0) * qscale, A_ROT_DIM // 2)
    for col in range(tn // LANES):
        sl = slice(col * LANES, (col + 1) * LANES)
        a = _rope_tile(acc_ref[:, sl], c, s_lo, s_hi, A_ROT_DIM // 2).astype(BF16)
        for r in range(d):
            o_ref[r, :, sl] = a[r * sub:(r + 1) * sub]

    acc_ref[...] = jnp.dot(xn_ref[...], w_ref[...], preferred_element_type=F32)


def _a_qkv(x, gain, w_qkv, group, d):
    s = x.shape[0]
    tm, tn = QKV_TM, QKV_TN
    ncol = 3 * A_WIDTH // tn
    nsteps = (s // tm) * ncol + 1
    cur = lambda g: jnp.minimum(g, nsteps - 2)
    prev = lambda g: jnp.maximum(g - 1, 0)
    tab_spec = pl.BlockSpec((tm, LANES), lambda g: (prev(g) // ncol, 0))
    offsets = np.arange(tm).reshape(tm // d, d).T.reshape(-1)
    tabs = _rope_tables(s, tm, offsets, _lane_freq(A_ROT_DIM))
    scratch = [pltpu.VMEM((tm, D_MODEL), BF16), pltpu.VMEM((tm, tn), F32)]
    if d > 1:
        scratch.append(pltpu.VMEM((QKV_XSLABS, tm, LANES), F32))
    return pl.pallas_call(
        functools.partial(_a_qkv_kernel, d=d, tm=tm, tn=tn, ncol=ncol, nsteps=nsteps),
        grid=(nsteps,),
        in_specs=[
            pl.BlockSpec((tm, D_MODEL), lambda g: (cur(g) // ncol, 0)),
            pl.BlockSpec((1, D_MODEL), lambda g: (0, 0)),
            pl.BlockSpec((D_MODEL, tn), lambda g: (0, group * ncol + cur(g) % ncol)),
            tab_spec, tab_spec,
        ],
        out_specs=pl.BlockSpec((d, tm // d, tn), lambda g: (0, prev(g) // ncol, prev(g) % ncol)),
        out_shape=jax.ShapeDtypeStruct((d, s // d, 3 * A_WIDTH), BF16),
        scratch_shapes=scratch,
        compiler_params=_params("arbitrary"),
        name=f"a_qkv_d{d}",
    )(x, gain, w_qkv, *tabs)


def _a_attn_kernel(q_ref, kp_ref, kc_ref, vp_ref, vc_ref, o_ref, lse_ref, *, tq):
    n = pl.program_id(1)
    blk = A_SPAN_BLK
    qi = lax.broadcasted_iota(jnp.int32, (blk, 2 * blk), 0)
    kj = lax.broadcasted_iota(jnp.int32, (blk, 2 * blk), 1)
    band = (kj >= qi) & (kj <= qi + blk)
    band_first = band & ((kj >= blk) | (n > 0))
    lane = lax.broadcasted_iota(jnp.int32, (blk, LANES), 1)
    ones = jnp.ones((2 * blk, LANES), BF16)
    nt = (((1,), (1,)), ((), ()))
    for b in range(tq // blk):
        rows = slice(b * blk, (b + 1) * blk)
        lse_tile = jnp.zeros((blk, LANES), F32)
        for h in range(A_HEADS):
            sl = slice(h * A_HEAD_DIM, (h + 1) * A_HEAD_DIM)
            if b == 0:
                k = jnp.concatenate([kp_ref[:, sl], kc_ref[0:blk, sl]], axis=0)
                v = jnp.concatenate([vp_ref[:, sl], vc_ref[0:blk, sl]], axis=0)
            else:
                k = kc_ref[(b - 1) * blk:(b + 1) * blk, sl]
                v = vc_ref[(b - 1) * blk:(b + 1) * blk, sl]
            s = lax.dot_general(q_ref[rows, sl], k, nt, preferred_element_type=F32)
            s = jnp.where(band_first if b == 0 else band, s, -jnp.inf)
            m = jnp.max(jnp.maximum(s[:, :blk], s[:, blk:]), axis=-1, keepdims=True)
            p = jnp.exp2(s - m).astype(BF16)
            oe = jnp.dot(p, jnp.concatenate([v, ones], axis=1), preferred_element_type=F32)
            l = oe[:, A_HEAD_DIM:]
            o_ref[rows, sl] = (oe[:, :A_HEAD_DIM] / l).astype(o_ref.dtype)
            lse = (m + jnp.log2(l)) * math.log(2.0)
            lse_tile = jnp.where(lane == h, lse, lse_tile)
        lse_ref[rows, :] = lse_tile


def _a_attn(qkv, d):
    _, m, _ = qkv.shape
    tq = A_ATT_TQ
    ratio = tq // A_SPAN_BLK
    cur = lambda col: pl.BlockSpec((None, tq, A_WIDTH), lambda r, n: (r, n, col))
    prev = lambda col: pl.BlockSpec(
        (None, A_SPAN_BLK, A_WIDTH), lambda r, n: (r, jnp.maximum(n * ratio - 1, 0), col))
    return pl.pallas_call(
        functools.partial(_a_attn_kernel, tq=tq),
        grid=(d, m // tq),
        in_specs=[cur(0), prev(1), cur(1), prev(2), cur(2)],
        out_specs=[
            pl.BlockSpec((None, tq, A_WIDTH), lambda r, n: (r, n, 0)),
            pl.BlockSpec((None, tq, LANES), lambda r, n: (r, n, 0)),
        ],
        out_shape=[
            jax.ShapeDtypeStruct((d, m, A_WIDTH), BF16),
            jax.ShapeDtypeStruct((d, m, LANES), F32),
        ],
        compiler_params=_params("parallel", "arbitrary"),
        name=f"a_attn_d{d}",
    )(qkv, qkv, qkv, qkv, qkv)


def _a_wo_kernel(o1_ref, o2_ref, o3_ref, l1_ref, l2_ref, l3_ref, h_ref, w_ref, out_ref,
                 nat2_ref, nat3_ref, ln2_ref, ln3_ref, xb0_ref, xb1_ref, *, tm, ntiles, ncol):
    g = pl.program_id(0)
    heads_per_piece = A_HEADS // ncol
    tn = D_MODEL // ncol

    @pl.when(g == 0)
    def _():
        xb1_ref[...] = jnp.zeros(xb1_ref.shape, BF16)

    def body(xb_new, xb_old):
        for o_ref, l_ref, nat_ref, ln_ref, (_, d) in (
                (o2_ref, l2_ref, nat2_ref, ln2_ref, A_GROUPS[1]),
                (o3_ref, l3_ref, nat3_ref, ln3_ref, A_GROUPS[2])):
            for r in range(d):
                rows = pl.ds(r, tm // d, stride=d)
                ln_ref[0, rows, :] = l_ref[r]
                for h in range(A_HEADS):
                    nat_ref[h, rows, :] = o_ref[r, :, h * A_HEAD_DIM:(h + 1) * A_HEAD_DIM].astype(F32)
        l1, l2, l3 = l1_ref[0], ln2_ref[0], ln3_ref[0]
        mx = jnp.maximum(jnp.maximum(l1, l2), l3)
        e1, e2, e3 = jnp.exp(l1 - mx), jnp.exp(l2 - mx), jnp.exp(l3 - mx)
        den = e1 + e2 + e3
        w2, w3 = e2 / den, e3 / den
        for piece in range(ncol):
            for h in range(piece * heads_per_piece, (piece + 1) * heads_per_piece):
                sl = slice(h * A_HEAD_DIM, (h + 1) * A_HEAD_DIM)
                o1 = o1_ref[0, :, sl].astype(F32)
                merged = (o1 + w2[:, h:h + 1] * (nat2_ref[h] - o1)
                          + w3[:, h:h + 1] * (nat3_ref[h] - o1))
                xb_new[:, sl] = merged.astype(BF16)
            cols = slice(piece * tn, (piece + 1) * tn)
            out_ref[:, cols] = h_ref[:, cols] + jnp.dot(xb_old[...], w_ref[:, cols],
                                                         preferred_element_type=F32)

    for parity, bufs in enumerate(((xb0_ref, xb1_ref), (xb1_ref, xb0_ref))):
        @pl.when(g % 2 == parity)
        def _(bufs=bufs):
            body(*bufs)


def _a_wo(outs, lses, h, w_o):
    s = h.shape[0]
    tm = A_WO_TM
    ntiles = s // tm
    cur = lambda g: jnp.minimum(g, ntiles - 1)
    prev = lambda g: jnp.maximum(g - 1, 0)
    o_spec = lambda d: pl.BlockSpec((d, tm // d, A_WIDTH), lambda g: (0, cur(g), 0))
    l_spec = lambda d: pl.BlockSpec((d, tm // d, LANES), lambda g: (0, cur(g), 0))
    ds_ = [d for _, d in A_GROUPS]
    return pl.pallas_call(
        functools.partial(_a_wo_kernel, tm=tm, ntiles=ntiles, ncol=A_WO_NCOL),
        grid=(ntiles + 1,),
        in_specs=[o_spec(ds_[0]), o_spec(ds_[1]), o_spec(ds_[2]),
                  l_spec(ds_[0]), l_spec(ds_[1]), l_spec(ds_[2]),
                  pl.BlockSpec((tm, D_MODEL), lambda g: (prev(g), 0)),
                  pl.BlockSpec((A_WIDTH, D_MODEL), lambda g: (0, 0))],
        out_specs=pl.BlockSpec((tm, D_MODEL), lambda g: (prev(g), 0)),
        out_shape=jax.ShapeDtypeStruct((s, D_MODEL), F32),
        scratch_shapes=[
            pltpu.VMEM((A_HEADS, tm, LANES), F32),
            pltpu.VMEM((A_HEADS, tm, LANES), F32),
            pltpu.VMEM((1, tm, LANES), F32),
            pltpu.VMEM((1, tm, LANES), F32),
            pltpu.VMEM((tm, A_WIDTH), BF16),
            pltpu.VMEM((tm, A_WIDTH), BF16),
        ],
        compiler_params=_params("arbitrary"),
        name="a_wo",
    )(*outs, *lses, h, w_o)


def _ffn_kernel(h_ref, g_ref, wg_ref, wu_ref, wd_ref, fg_ref, out_ref, xn_ref, *, final_norm):
    f = pl.program_id(1)

    @pl.when(f == 0)
    def _():
        xn_ref[...] = (_rms_scale(h_ref[...]) * g_ref[...]).astype(BF16)
        out_ref[...] = jnp.zeros(out_ref.shape, F32)

    xn = xn_ref[...]
    gate = jnp.dot(xn, wg_ref[...], preferred_element_type=F32)
    up = jnp.dot(xn, wu_ref[...], preferred_element_type=F32)
    act = (gate * jax.nn.sigmoid(gate) * up).astype(BF16)
    out_ref[...] += jnp.dot(act, wd_ref[...], preferred_element_type=F32)

    @pl.when(f == pl.num_programs(1) - 1)
    def _():
        y = h_ref[...] + out_ref[...]
        if final_norm:
            y = _rms_scale(y) * fg_ref[...]
        out_ref[...] = y


def _ffn(h, gain, w_gu, w_down, layer, final_gain, final_norm):
    s = h.shape[0]
    tm, tf = FFN_TM, FFN_TF
    nf = FFN_HIDDEN // tf
    return pl.pallas_call(
        functools.partial(_ffn_kernel, final_norm=final_norm),
        grid=(s // tm, nf),
        in_specs=[
            pl.BlockSpec((tm, D_MODEL), lambda i, f: (i, 0)),
            pl.BlockSpec((1, D_MODEL), lambda i, f: (0, 0)),
            pl.BlockSpec((None, D_MODEL, tf), lambda i, f: (layer, 0, f)),
            pl.BlockSpec((None, D_MODEL, tf), lambda i, f: (layer, 0, nf + f)),
            pl.BlockSpec((None, tf, D_MODEL), lambda i, f: (layer, f, 0)),
            pl.BlockSpec((1, D_MODEL), lambda i, f: (0, 0)),
        ],
        out_specs=pl.BlockSpec((tm, D_MODEL), lambda i, f: (i, 0)),
        out_shape=jax.ShapeDtypeStruct((s, D_MODEL), F32),
        scratch_shapes=[pltpu.VMEM((tm, D_MODEL), BF16)],
        compiler_params=_params("parallel", "arbitrary"),
        name="ffn_final" if final_norm else "ffn",
    )(h, gain, w_gu, w_gu, w_down, final_gain)


def _b_proj_kernel(h_ref, ga_ref, gk_ref, wqa_ref, wkva_ref, gqa_ref, gkva_ref, wqb_ref, wkvb_ref,
                   c_ref, s_ref, ct_ref, st_ref, qt_ref, k_ref, vt_ref):
    y = _rms_scale(h_ref[...])
    xa = (y * ga_ref[...]).astype(BF16)
    xk = (y * gk_ref[...]).astype(BF16)
    c = c_ref[...]
    s_lo, s_hi = _split_sin(s_ref[...], B_ROPE // 2)

    cq = jnp.dot(xa, wqa_ref[...], preferred_element_type=F32)
    cq = (_rms_scale(cq) * gqa_ref[...]).astype(BF16)
    qt = lax.dot_general(wqb_ref[...], cq, (((1,), (1,)), ((), ())),
                         preferred_element_type=F32)
    qscale = (B_NOPE + B_ROPE) ** -0.5 * math.log2(math.e)
    cos_t, sin_t = ct_ref[...], st_ref[...]
    half = B_ROPE // 2
    for h in range(B_HEADS):
        lo = h * B_QK_PAD
        src = h * (B_NOPE + B_ROPE)
        qt_ref[lo:lo + B_NOPE, :] = (qt[src:src + B_NOPE] * qscale).astype(BF16)
        x1 = qt[src + B_NOPE:src + B_NOPE + half]
        x2 = qt[src + B_NOPE + half:src + B_NOPE + B_ROPE]
        qt_ref[lo + B_NOPE:lo + B_NOPE + half, :] = (
            (x1 * cos_t - x2 * sin_t) * qscale).astype(BF16)
        qt_ref[lo + B_NOPE + half:lo + B_NOPE + B_ROPE, :] = (
            (x2 * cos_t + x1 * sin_t) * qscale).astype(BF16)
        qt_ref[lo + B_NOPE + B_ROPE:lo + B_QK_PAD, :] = jnp.zeros(
            (B_QK_PAD - B_NOPE - B_ROPE, qt.shape[1]), BF16)

    ckv = jnp.dot(xk, wkva_ref[...], preferred_element_type=F32)
    lat = (_rms_scale(ckv[:, :B_KV_LORA]) * gkva_ref[...]).astype(BF16)
    k_pe = _rope_tile(ckv[:, B_KV_LORA:], c, s_lo, s_hi, B_ROPE // 2).astype(BF16)
    kv = jnp.dot(lat, wkvb_ref[...], preferred_element_type=F32)
    for h in range(B_HEADS):
        lo = h * B_QK_PAD
        k_ref[:, lo:lo + B_NOPE] = kv[:, lo:lo + B_NOPE].astype(BF16)
        k_ref[:, lo + B_NOPE:lo + B_QK_PAD] = k_pe
        vt_ref[h * B_VE:h * B_VE + B_V, :] = kv[:, lo + B_NOPE:lo + B_QK_PAD].T.astype(BF16)
        vt_ref[h * B_VE + B_V:(h + 1) * B_VE, :] = jnp.ones((B_VE - B_V, h_ref.shape[0]), BF16)


def _b_proj(h, g_attn, g_kv, w_q_a, w_kv_a, g_qa, g_kva, w_q_b_t, w_kv_b, tabs, tabs_t):
    s = h.shape[0]
    tm = B_PROJ_TM
    full = lambda a: pl.BlockSpec(a.shape, lambda i: (0,) * a.ndim)
    row = lambda width: pl.BlockSpec((tm, width), lambda i: (i, 0))
    col = lambda height: pl.BlockSpec((height, tm), lambda i: (0, i))
    return pl.pallas_call(
        _b_proj_kernel,
        grid=(s // tm,),
        in_specs=[row(D_MODEL), full(g_attn), full(g_kv), full(w_q_a), full(w_kv_a), full(g_qa),
                  full(g_kva), full(w_q_b_t), full(w_kv_b), row(LANES), row(LANES),
                  col(B_ROPE // 2), col(B_ROPE // 2)],
        out_specs=[col(B_HEADS * B_QK_PAD), row(B_HEADS * B_QK_PAD), col(B_HEADS * B_VE)],
        out_shape=[
            jax.ShapeDtypeStruct((B_HEADS * B_QK_PAD, s), BF16),
            jax.ShapeDtypeStruct((s, B_HEADS * B_QK_PAD), BF16),
            jax.ShapeDtypeStruct((B_HEADS * B_VE, s), BF16),
        ],
        compiler_params=_params("parallel"),
        name="b_proj",
    )(h, g_attn, g_kv, w_q_a, w_kv_a, g_qa, g_kva, w_q_b_t, w_kv_b, *tabs, *tabs_t)


def _b_attn_kernel(qi_ref, ki_ref, hh_ref, qt_ref, k_ref, vt_ref, o_ref, s_ref, m_ref,
                   alpha_ref, acc_ref, *, t, cq, kb, kp, nitems):
    g = pl.program_id(0)
    nchunk = t // cq
    a = jnp.minimum(g, nitems - 1)
    b = jnp.maximum(g - 1, 0)
    qi_a, ki_a = qi_ref[a], ki_ref[a]

    @pl.when(g == 0)
    def _():
        s_ref[...] = jnp.zeros(s_ref.shape, F32)
        m_ref[...] = jnp.zeros(m_ref.shape, F32)
        alpha_ref[...] = jnp.zeros(alpha_ref.shape, F32)
        acc_ref[...] = jnp.zeros(acc_ref.shape, F32)

    def body(diag_a, diag_b):
        for c in range(nchunk):
            m_b = m_ref[c]
            rows_a = (c + 1) * cq if diag_a else t
            rows_b = (c + 1) * cq if diag_b else t
            cmax, pv = None, None
            for i in range(t // kp):
                for j in range(i * kp // kb, min((i + 1) * kp, rows_b) // kb):
                    pt = jnp.exp2(s_ref[c, j * kb:(j + 1) * kb, :] - m_b)
                    d = jnp.dot(vt_ref[:, j * kb:(j + 1) * kb], pt.astype(BF16),
                                preferred_element_type=F32)
                    pv = d if pv is None else pv + d
                if i * kp < rows_a:
                    s = jnp.dot(k_ref[i * kp:(i + 1) * kp, :], qt_ref[:, c * cq:(c + 1) * cq],
                                preferred_element_type=F32)
                    if diag_a and (i + 1) * kp > c * cq:
                        k_pos = i * kp + lax.broadcasted_iota(jnp.int32, (kp, cq), 0)
                        q_pos = c * cq + lax.broadcasted_iota(jnp.int32, (kp, cq), 1)
                        s = jnp.where(k_pos <= q_pos, s, -jnp.inf)
                    s_ref[c, i * kp:(i + 1) * kp, :] = s
                    pmax = jnp.max(s, axis=0, keepdims=True)
                    cmax = pmax if cmax is None else jnp.maximum(cmax, pmax)
            acc_ref[c] = alpha_ref[c] * acc_ref[c] + pv
            m_prev = jnp.where(ki_a == 0, -jnp.inf, m_b)
            m_new = jnp.maximum(m_prev, cmax)
            m_ref[c] = m_new
            alpha_ref[c] = jnp.exp2(m_prev - m_new)

    is_diag_b = ki_ref[b] == qi_ref[b]
    for diag_a in (False, True):
        for diag_b in (False, True):
            @pl.when(((ki_a == qi_a) == diag_a) & (is_diag_b == diag_b))
            def _(diag_a=diag_a, diag_b=diag_b):
                body(diag_a, diag_b)

    @pl.when(is_diag_b)
    def _():
        for c in range(nchunk):
            acc = acc_ref[c]
            o = acc[:B_V] / acc[B_V:B_V + 1]
            o_ref[c * cq:(c + 1) * cq, :] = o.T.astype(o_ref.dtype)


def _b_attn(qt, k, vt):
    s = k.shape[0]
    t, cq = B_ATT_T, B_ATT_CQ
    items = [(h, a, b) for h in range(B_HEADS) for a in range(s // t) for b in range(a + 1)]
    n = len(items)
    hh, qi, ki = (jnp.asarray(np.array(col, np.int32)) for col in zip(*items))
    cur = lambda g: jnp.minimum(g, n - 1)
    prev = lambda g: jnp.maximum(g - 1, 0)
    nchunk = t // cq
    grid_spec = pltpu.PrefetchScalarGridSpec(
        num_scalar_prefetch=3,
        grid=(n + 1,),
        in_specs=[
            pl.BlockSpec((B_QK_PAD, t), lambda g, qi, ki, hh: (hh[cur(g)], qi[cur(g)])),
            pl.BlockSpec((t, B_QK_PAD), lambda g, qi, ki, hh: (ki[cur(g)], hh[cur(g)])),
            pl.BlockSpec((B_VE, t), lambda g, qi, ki, hh: (hh[prev(g)], ki[prev(g)])),
        ],
        out_specs=pl.BlockSpec((t, B_V), lambda g, qi, ki, hh: (qi[prev(g)], hh[prev(g)])),
        scratch_shapes=[
            pltpu.VMEM((nchunk, t, cq), F32),
            pltpu.VMEM((nchunk, 1, cq), F32),
            pltpu.VMEM((nchunk, 1, cq), F32),
            pltpu.VMEM((nchunk, B_VE, cq), F32),
        ],
    )
    return pl.pallas_call(
        functools.partial(_b_attn_kernel, t=t, cq=cq, kb=B_ATT_KB, kp=B_ATT_KP, nitems=n),
        grid_spec=grid_spec,
        out_shape=jax.ShapeDtypeStruct((s, B_HEADS * B_V), BF16),
        compiler_params=_params("arbitrary"),
        name="b_attn",
    )(qi, ki, hh, qt, k, vt)


def _b_wo_kernel(o_ref, h_ref, w_ref, out_ref):
    out_ref[...] = h_ref[...] + jnp.dot(o_ref[...], w_ref[...], preferred_element_type=F32)


def _b_wo(o, h, w_o):
    s = h.shape[0]
    tm = B_WO_TM
    return pl.pallas_call(
        _b_wo_kernel,
        grid=(s // tm,),
        in_specs=[pl.BlockSpec((tm, B_HEADS * B_V), lambda i: (i, 0)),
                  pl.BlockSpec((tm, D_MODEL), lambda i: (i, 0)),
                  pl.BlockSpec((B_HEADS * B_V, D_MODEL), lambda i: (0, 0))],
        out_specs=pl.BlockSpec((tm, D_MODEL), lambda i: (i, 0)),
        out_shape=jax.ShapeDtypeStruct((s, D_MODEL), F32),
        compiler_params=_params("parallel"),
        name="b_wo",
    )(o, h, w_o)


def _rope_freq(dim):
    return ROPE_THETA ** (-jnp.arange(0, dim, 2, dtype=F32) / dim)


def _lane_freq(dim):
    f = _rope_freq(dim)
    return jnp.concatenate([f, f, jnp.zeros((LANES - dim,), F32)])


def _angle_parts(seq_len, tile, offsets, freq):
    start = (jnp.arange(seq_len // tile, dtype=F32) * tile)[:, None] * freq[None, :]
    off = jnp.asarray(offsets, F32)[:, None] * freq[None, :]
    return jnp.cos(start), jnp.sin(start), jnp.cos(off), jnp.sin(off)


def _rope_tables(seq_len, tile, offsets, freq):
    ca, sa, cb, sb = _angle_parts(seq_len, tile, offsets, freq)
    cos = ca[:, None, :] * cb[None] - sa[:, None, :] * sb[None]
    sin = sa[:, None, :] * cb[None] + ca[:, None, :] * sb[None]
    return cos.reshape(seq_len, -1), sin.reshape(seq_len, -1)


def _rope_tables_t(seq_len, tile, freq):
    ca, sa, cb, sb = (t.T for t in _angle_parts(seq_len, tile, np.arange(tile), freq))
    cos = ca[:, :, None] * cb[:, None, :] - sa[:, :, None] * sb[:, None, :]
    sin = sa[:, :, None] * cb[:, None, :] + ca[:, :, None] * sb[:, None, :]
    return cos.reshape(-1, seq_len), sin.reshape(-1, seq_len)


def kernel(x, attn_norm_g, ffn_norm_g, a_w_qkv, a_w_o, kv_norm_g, b_w_kv_a, b_kv_a_norm_g, b_w_kv_b,
           b_w_q_a, b_q_a_norm_g, b_w_q_b, b_w_o, ffn_w_gu, ffn_w_down, final_norm_g):
    assert x.shape[0] == 1 and x.shape[2] == D_MODEL
    s = x.shape[1]
    assert s % QKV_TM == 0 and s % (A_GROUPS[-1][1] * A_ATT_TQ) == 0 and s % B_ATT_T == 0
    h = x[0]
    row = lambda g: g.reshape(1, -1).astype(F32)
    tabs_b = _rope_tables(s, B_PROJ_TM, np.arange(B_PROJ_TM), _lane_freq(B_ROPE))
    tabs_bt = _rope_tables_t(s, B_PROJ_TM, _rope_freq(B_ROPE))

    w_qkv = a_w_qkv[0].astype(BF16)
    outs, lses = [], []
    for group, (_, d) in enumerate(A_GROUPS):
        qkv = _a_qkv(h, row(attn_norm_g[0]), w_qkv, group, d)
        o, lse = _a_attn(qkv, d)
        outs.append(o)
        lses.append(lse)
    h = _a_wo(outs, lses, h, a_w_o[0].astype(BF16))
    w_gu, w_down = ffn_w_gu.astype(BF16), ffn_w_down.astype(BF16)
    h = _ffn(h, row(ffn_norm_g[0]), w_gu, w_down, 0, row(final_norm_g), final_norm=False)

    w_kv_a = jnp.pad(b_w_kv_a, ((0, 0), (0, LANES - B_ROPE))).astype(BF16)
    qt, k, vt = _b_proj(h, row(attn_norm_g[1]), row(kv_norm_g), b_w_q_a[0].astype(BF16), w_kv_a,
                        row(b_q_a_norm_g[0]), row(b_kv_a_norm_g), b_w_q_b[0].T.astype(BF16),
                        b_w_kv_b.astype(BF16), tabs_b, tabs_bt)
    o = _b_attn(qt, k, vt)
    h = _b_wo(o, h, b_w_o[0].astype(BF16))
    h = _ffn(h, row(ffn_norm_g[1]), w_gu, w_down, 1, row(final_norm_g), final_norm=True)
    return h[None]
```

```python
import functools
import math

import jax
import jax.numpy as jnp
import numpy as np
from jax import lax
from jax.experimental import pallas as pl
from jax.experimental.pallas import tpu as pltpu

F32 = jnp.float32
BF16 = jnp.bfloat16

D_MODEL = 2048
ROPE_THETA = 500000.0
NORM_EPS = 1e-6
LANES = 128

A_HEAD_DIM = 128
A_HEADS = D_MODEL // A_HEAD_DIM
A_ROT_DIM = A_HEAD_DIM // 4
A_GROUPS = ((128, 1), (512, 4), (2048, 16))
A_WIDTH = A_HEADS * A_HEAD_DIM
A_SPAN_BLK = 128

B_HEADS = D_MODEL // 128
B_NOPE = 128
B_ROPE = 64
B_V = 128
B_VE = B_V + 16
B_KV_LORA = 512
B_QK_PAD = 256

FFN_HIDDEN = 5632

VMEM_LIMIT = 56 * 1024 * 1024

QKV_TM, QKV_TN = 1024, 1024
QKV_XSLABS = 4
A_ATT_TQ = 512
A_WO_TM = 256
A_WO_NCOL = 4
FFN_TM, FFN_TF = 512, 512
B_PROJ_TM = 256
B_ATT_T, B_ATT_CQ = 2048, 512
B_ATT_KB = 256
B_ATT_KP = 512
B_WO_TM = 512


def _params(*sem):
    return pltpu.CompilerParams(dimension_semantics=sem, vmem_limit_bytes=VMEM_LIMIT)


def _rms_scale(x):
    return x * lax.rsqrt(jnp.mean(x * x, axis=-1, keepdims=True) + NORM_EPS)


def _split_sin(s, half):
    lane = lax.broadcasted_iota(jnp.int32, s.shape, 1)
    return jnp.where(lane < half, -s, 0.0), jnp.where(lane >= half, s, 0.0)


def _rope_tile(a, c, s_lo, s_hi, half):
    return (a * c + pltpu.roll(a, LANES - half, 1) * s_lo + pltpu.roll(a, half, 1) * s_hi)


def _a_qkv_kernel(x_ref, g_ref, w_ref, c_ref, s_ref, o_ref, xn_ref, acc_ref, *xs,
                  d, tm, tn, ncol, nsteps):
    g = pl.program_id(0)
    sub = tm // d

    @pl.when(g == 0)
    def _():
        acc_ref[...] = jnp.zeros(acc_ref.shape, F32)

    @pl.when((g % ncol == 0) & (g < nsteps - 1))
    def _():
        if d == 1:
            xn_ref[...] = (_rms_scale(x_ref[...]) * g_ref[...]).astype(BF16)
        else:
            x = x_ref[...]
            inv = lax.rsqrt(jnp.mean(x * x, axis=-1, keepdims=True) + NORM_EPS)
            nslab = xs[0].shape[0]
            for base in range(0, D_MODEL // LANES, nslab):
                for k in range(nslab):
                    cols = slice((base + k) * LANES, (base + k + 1) * LANES)
                    xs[0][k] = x_ref[:, cols] * inv * g_ref[:, cols]
                for k in range(nslab):
                    for r in range(d):
                        xn_ref[r * sub:(r + 1) * sub, (base + k) * LANES:(base + k + 1) * LANES] = (
                            xs[0][k, pl.ds(r, sub, stride=d), :].astype(BF16))

    t = (jnp.maximum(g - 1, 0) % ncol) // (A_WIDTH // tn)
    rot = t < 2
    qscale = jnp.where(t == 0, A_HEAD_DIM ** -0.5 * math.log2(math.e), 1.0).astype(F32)
    c = jnp.where(rot, c_ref[...], 1.0) * qscale
    s_lo, s_hi = _split_sin(jnp.where(rot, s_ref[...], 0.0) * qscale, A_ROT_DIM // 2)
    for col in range(tn // LANES):
        sl = slice(col * LANES, (col + 1) * LANES)
        a = _rope_tile(acc_ref[:, sl], c, s_lo, s_hi, A_ROT_DIM // 2).astype(BF16)
        for r in range(d):
            o_ref[r, :, sl] = a[r * sub:(r + 1) * sub]

    acc_ref[...] = jnp.dot(xn_ref[...], w_ref[...], preferred_element_type=F32)


def _a_qkv(x, gain, w_qkv, group, d):
    s = x.shape[0]
    tm, tn = QKV_TM, QKV_TN
    ncol = 3 * A_WIDTH // tn
    nsteps = (s // tm) * ncol + 1
    cur = lambda g: jnp.minimum(g, nsteps - 2)
    prev = lambda g: jnp.maximum(g - 1, 0)
    tab_spec = pl.BlockSpec((tm, LANES), lambda g: (prev(g) // ncol, 0))
    offsets = np.arange(tm).reshape(tm // d, d).T.reshape(-1)
    tabs = _rope_tables(s, tm, offsets, _lane_freq(A_ROT_DIM))
    scratch = [pltpu.VMEM((tm, D_MODEL), BF16), pltpu.VMEM((tm, tn), F32)]
    if d > 1:
        scratch.append(pltpu.VMEM((QKV_XSLABS, tm, LANES), F32))
    return pl.pallas_call(
        functools.partial(_a_qkv_kernel, d=d, tm=tm, tn=tn, ncol=ncol, nsteps=nsteps),
        grid=(nsteps,),
        in_specs=[
            pl.BlockSpec((tm, D_MODEL), lambda g: (cur(g) // ncol, 0)),
            pl.BlockSpec((1, D_MODEL), lambda g: (0, 0)),
            pl.BlockSpec((D_MODEL, tn), lambda g: (0, group * ncol + cur(g) % ncol)),
            tab_spec, tab_spec,
        ],
        out_specs=pl.BlockSpec((d, tm // d, tn), lambda g: (0, prev(g) // ncol, prev(g) % ncol)),
        out_shape=jax.ShapeDtypeStruct((d, s // d, 3 * A_WIDTH), BF16),
        scratch_shapes=scratch,
        compiler_params=_params("arbitrary"),
        name=f"a_qkv_d{d}",
    )(x, gain, w_qkv, *tabs)


def _a_attn_kernel(q_ref, kp_ref, kc_ref, vp_ref, vc_ref, o_ref, lse_ref, *, tq):
    n = pl.program_id(1)
    blk = A_SPAN_BLK
    qi = lax.broadcasted_iota(jnp.int32, (blk, 2 * blk), 0)
    kj = lax.broadcasted_iota(jnp.int32, (blk, 2 * blk), 1)
    band = (kj >= qi) & (kj <= qi + blk)
    band_first = band & ((kj >= blk) | (n > 0))
    lane = lax.broadcasted_iota(jnp.int32, (blk, LANES), 1)
    ones = jnp.ones((2 * blk, LANES), BF16)
    nt = (((1,), (1,)), ((), ()))
    for b in range(tq // blk):
        rows = slice(b * blk, (b + 1) * blk)
        lse_tile = jnp.zeros((blk, LANES), F32)
        for h in range(A_HEADS):
            sl = slice(h * A_HEAD_DIM, (h + 1) * A_HEAD_DIM)
            if b == 0:
                k = jnp.concatenate([kp_ref[:, sl], kc_ref[0:blk, sl]], axis=0)
                v = jnp.concatenate([vp_ref[:, sl], vc_ref[0:blk, sl]], axis=0)
            else:
                k = kc_ref[(b - 1) * blk:(b + 1) * blk, sl]
                v = vc_ref[(b - 1) * blk:(b + 1) * blk, sl]
            s = lax.dot_general(q_ref[rows, sl], k, nt, preferred_element_type=F32)
            s = jnp.where(band_first if b == 0 else band, s, -jnp.inf)
            m = jnp.max(jnp.maximum(s[:, :blk], s[:, blk:]), axis=-1, keepdims=True)
            p = jnp.exp2(s - m).astype(BF16)
            oe = jnp.dot(p, jnp.concatenate([v, ones], axis=1), preferred_element_type=F32)
            l = oe[:, A_HEAD_DIM:]
            o_ref[rows, sl] = (oe[:, :A_HEAD_DIM] / l).astype(o_ref.dtype)
            lse = (m + jnp.log2(l)) * math.log(2.0)
            lse_tile = jnp.where(lane == h, lse, lse_tile)
        lse_ref[rows, :] = lse_tile


def _a_attn(qkv, d):
    _, m, _ = qkv.shape
    tq = A_ATT_TQ
    ratio = tq // A_SPAN_BLK
    cur = lambda col: pl.BlockSpec((None, tq, A_WIDTH), lambda r, n: (r, n, col))
    prev = lambda col: pl.BlockSpec(
        (None, A_SPAN_BLK, A_WIDTH), lambda r, n: (r, jnp.maximum(n * ratio - 1, 0), col))
    return pl.pallas_call(
        functools.partial(_a_attn_kernel, tq=tq),
        grid=(d, m // tq),
        in_specs=[cur(0), prev(1), cur(1), prev(2), cur(2)],
        out_specs=[
            pl.BlockSpec((None, tq, A_WIDTH), lambda r, n: (r, n, 0)),
            pl.BlockSpec((None, tq, LANES), lambda r, n: (r, n, 0)),
        ],
        out_shape=[
            jax.ShapeDtypeStruct((d, m, A_WIDTH), BF16),
            jax.ShapeDtypeStruct((d, m, LANES), F32),
        ],
        compiler_params=_params("parallel", "arbitrary"),
        name=f"a_attn_d{d}",
    )(qkv, qkv, qkv, qkv, qkv)


def _a_wo_kernel(o1_ref, o2_ref, o3_ref, l1_ref, l2_ref, l3_ref, h_ref, w_ref, out_ref,
                 nat2_ref, nat3_ref, ln2_ref, ln3_ref, xb0_ref, xb1_ref, *, tm, ncol):
    g = pl.program_id(0)
    heads_per_piece = A_HEADS // ncol
    tn = D_MODEL // ncol

    @pl.when(g == 0)
    def _():
        xb1_ref[...] = jnp.zeros(xb1_ref.shape, BF16)

    def body(xb_new, xb_old):
        for o_ref, l_ref, nat_ref, ln_ref, (_, d) in (
                (o2_ref, l2_ref, nat2_ref, ln2_ref, A_GROUPS[1]),
                (o3_ref, l3_ref, nat3_ref, ln3_ref, A_GROUPS[2])):
            for r in range(d):
                rows = pl.ds(r, tm // d, stride=d)
                ln_ref[0, rows, :] = l_ref[r]
                for h in range(A_HEADS):
                    nat_ref[h, rows, :] = o_ref[r, :, h * A_HEAD_DIM:(h + 1) * A_HEAD_DIM].astype(F32)
        l1, l2, l3 = l1_ref[0], ln2_ref[0], ln3_ref[0]
        mx = jnp.maximum(jnp.maximum(l1, l2), l3)
        e1, e2, e3 = jnp.exp(l1 - mx), jnp.exp(l2 - mx), jnp.exp(l3 - mx)
        den = e1 + e2 + e3
        w2, w3 = e2 / den, e3 / den
        for piece in range(ncol):
            for h in range(piece * heads_per_piece, (piece + 1) * heads_per_piece):
                sl = slice(h * A_HEAD_DIM, (h + 1) * A_HEAD_DIM)
                o1 = o1_ref[0, :, sl].astype(F32)
                merged = (o1 + w2[:, h:h + 1] * (nat2_ref[h] - o1)
                          + w3[:, h:h + 1] * (nat3_ref[h] - o1))
                xb_new[:, sl] = merged.astype(BF16)
            cols = slice(piece * tn, (piece + 1) * tn)
            out_ref[:, cols] = h_ref[:, cols] + jnp.dot(xb_old[...], w_ref[:, cols],
                                                         preferred_element_type=F32)

    for parity, bufs in enumerate(((xb0_ref, xb1_ref), (xb1_ref, xb0_ref))):
        @pl.when(g % 2 == parity)
        def _(bufs=bufs):
            body(*bufs)


def _a_wo(outs, lses, h, w_o):
    s = h.shape[0]
    tm = A_WO_TM
    ntiles = s // tm
    cur = lambda g: jnp.minimum(g, ntiles - 1)
    prev = lambda g: jnp.maximum(g - 1, 0)
    o_spec = lambda d: pl.BlockSpec((d, tm // d, A_WIDTH), lambda g: (0, cur(g), 0))
    l_spec = lambda d: pl.BlockSpec((d, tm // d, LANES), lambda g: (0, cur(g), 0))
    ds_ = [d for _, d in A_GROUPS]
    return pl.pallas_call(
        functools.partial(_a_wo_kernel, tm=tm, ncol=A_WO_NCOL),
        grid=(ntiles + 1,),
        in_specs=[o_spec(ds_[0]), o_spec(ds_[1]), o_spec(ds_[2]),
                  l_spec(ds_[0]), l_spec(ds_[1]), l_spec(ds_[2]),
                  pl.BlockSpec((tm, D_MODEL), lambda g: (prev(g), 0)),
                  pl.BlockSpec((A_WIDTH, D_MODEL), lambda g: (0, 0))],
        out_specs=pl.BlockSpec((tm, D_MODEL), lambda g: (prev(g), 0)),
        out_shape=jax.ShapeDtypeStruct((s, D_MODEL), F32),
        scratch_shapes=[
            pltpu.VMEM((A_HEADS, tm, LANES), F32),
            pltpu.VMEM((A_HEADS, tm, LANES), F32),
            pltpu.VMEM((1, tm, LANES), F32),
            pltpu.VMEM((1, tm, LANES), F32),
            pltpu.VMEM((tm, A_WIDTH), BF16),
            pltpu.VMEM((tm, A_WIDTH), BF16),
        ],
        compiler_params=_params("arbitrary"),
        name="a_wo",
    )(*outs, *lses, h, w_o)


def _ffn_kernel(h_ref, g_ref, wg_ref, wu_ref, wd_ref, fg_ref, out_ref, xn_ref, *, final_norm):
    f = pl.program_id(1)

    @pl.when(f == 0)
    def _():
        xn_ref[...] = (_rms_scale(h_ref[...]) * g_ref[...]).astype(BF16)
        out_ref[...] = jnp.zeros(out_ref.shape, F32)

    xn = xn_ref[...]
    gate = jnp.dot(xn, wg_ref[...], preferred_element_type=F32)
    up = jnp.dot(xn, wu_ref[...], preferred_element_type=F32)
    act = (gate * jax.nn.sigmoid(gate) * up).astype(BF16)
    out_ref[...] += jnp.dot(act, wd_ref[...], preferred_element_type=F32)

    @pl.when(f == pl.num_programs(1) - 1)
    def _():
        y = h_ref[...] + out_ref[...]
        if final_norm:
            y = _rms_scale(y) * fg_ref[...]
        out_ref[...] = y


def _ffn(h, gain, w_gu, w_down, layer, final_gain, final_norm):
    s = h.shape[0]
    tm, tf = FFN_TM, FFN_TF
    nf = FFN_HIDDEN // tf
    return pl.pallas_call(
        functools.partial(_ffn_kernel, final_norm=final_norm),
        grid=(s // tm, nf),
        in_specs=[
            pl.BlockSpec((tm, D_MODEL), lambda i, f: (i, 0)),
            pl.BlockSpec((1, D_MODEL), lambda i, f: (0, 0)),
            pl.BlockSpec((None, D_MODEL, tf), lambda i, f: (layer, 0, f)),
            pl.BlockSpec((None, D_MODEL, tf), lambda i, f: (layer, 0, nf + f)),
            pl.BlockSpec((None, tf, D_MODEL), lambda i, f: (layer, f, 0)),
            pl.BlockSpec((1, D_MODEL), lambda i, f: (0, 0)),
        ],
        out_specs=pl.BlockSpec((tm, D_MODEL), lambda i, f: (i, 0)),
        out_shape=jax.ShapeDtypeStruct((s, D_MODEL), F32),
        scratch_shapes=[pltpu.VMEM((tm, D_MODEL), BF16)],
        compiler_params=_params("parallel", "arbitrary"),
        name="ffn_final" if final_norm else "ffn",
    )(h, gain, w_gu, w_gu, w_down, final_gain)


def _b_proj_kernel(h_ref, ga_ref, gk_ref, wqa_ref, wkva_ref, gqa_ref, gkva_ref, wqb_ref, wkvb_ref,
                   c_ref, s_ref, ct_ref, st_ref, qt_ref, k_ref, vt_ref):
    y = _rms_scale(h_ref[...])
    xa = (y * ga_ref[...]).astype(BF16)
    xk = (y * gk_ref[...]).astype(BF16)
    c = c_ref[...]
    s_lo, s_hi = _split_sin(s_ref[...], B_ROPE // 2)

    cq = jnp.dot(xa, wqa_ref[...], preferred_element_type=F32)
    cq = (_rms_scale(cq) * gqa_ref[...]).astype(BF16)
    qt = lax.dot_general(wqb_ref[...], cq, (((1,), (1,)), ((), ())),
                         preferred_element_type=F32)
    qscale = (B_NOPE + B_ROPE) ** -0.5 * math.log2(math.e)
    cos_t, sin_t = ct_ref[...], st_ref[...]
    half = B_ROPE // 2
    for h in range(B_HEADS):
        lo = h * B_QK_PAD
        src = h * (B_NOPE + B_ROPE)
        qt_ref[lo:lo + B_NOPE, :] = (qt[src:src + B_NOPE] * qscale).astype(BF16)
        x1 = qt[src + B_NOPE:src + B_NOPE + half]
        x2 = qt[src + B_NOPE + half:src + B_NOPE + B_ROPE]
        qt_ref[lo + B_NOPE:lo + B_NOPE + half, :] = (
            (x1 * cos_t - x2 * sin_t) * qscale).astype(BF16)
        qt_ref[lo + B_NOPE + half:lo + B_NOPE + B_ROPE, :] = (
            (x2 * cos_t + x1 * sin_t) * qscale).astype(BF16)
        qt_ref[lo + B_NOPE + B_ROPE:lo + B_QK_PAD, :] = jnp.zeros(
            (B_QK_PAD - B_NOPE - B_ROPE, qt.shape[1]), BF16)

    ckv = jnp.dot(xk, wkva_ref[...], preferred_element_type=F32)
    lat = (_rms_scale(ckv[:, :B_KV_LORA]) * gkva_ref[...]).astype(BF16)
    k_pe = _rope_tile(ckv[:, B_KV_LORA:], c, s_lo, s_hi, B_ROPE // 2).astype(BF16)
    kv = jnp.dot(lat, wkvb_ref[...], preferred_element_type=F32)
    for h in range(B_HEADS):
        lo = h * B_QK_PAD
        k_ref[:, lo:lo + B_NOPE] = kv[:, lo:lo + B_NOPE].astype(BF16)
        k_ref[:, lo + B_NOPE:lo + B_QK_PAD] = k_pe
        vt_ref[h * B_VE:h * B_VE + B_V, :] = kv[:, lo + B_NOPE:lo + B_QK_PAD].T.astype(BF16)
        vt_ref[h * B_VE + B_V:(h + 1) * B_VE, :] = jnp.ones((B_VE - B_V, h_ref.shape[0]), BF16)


def _b_proj(h, g_attn, g_kv, w_q_a, w_kv_a, g_qa, g_kva, w_q_b_t, w_kv_b, tabs, tabs_t):
    s = h.shape[0]
    tm = B_PROJ_TM
    full = lambda a: pl.BlockSpec(a.shape, lambda i: (0,) * a.ndim)
    row = lambda width: pl.BlockSpec((tm, width), lambda i: (i, 0))
    col = lambda height: pl.BlockSpec((height, tm), lambda i: (0, i))
    return pl.pallas_call(
        _b_proj_kernel,
        grid=(s // tm,),
        in_specs=[row(D_MODEL), full(g_attn), full(g_kv), full(w_q_a), full(w_kv_a), full(g_qa),
                  full(g_kva), full(w_q_b_t), full(w_kv_b), row(LANES), row(LANES),
                  col(B_ROPE // 2), col(B_ROPE // 2)],
        out_specs=[col(B_HEADS * B_QK_PAD), row(B_HEADS * B_QK_PAD), col(B_HEADS * B_VE)],
        out_shape=[
            jax.ShapeDtypeStruct((B_HEADS * B_QK_PAD, s), BF16),
            jax.ShapeDtypeStruct((s, B_HEADS * B_QK_PAD), BF16),
            jax.ShapeDtypeStruct((B_HEADS * B_VE, s), BF16),
        ],
        compiler_params=_params("parallel"),
        name="b_proj",
    )(h, g_attn, g_kv, w_q_a, w_kv_a, g_qa, g_kva, w_q_b_t, w_kv_b, *tabs, *tabs_t)


def _b_attn_kernel(qi_ref, ki_ref, hh_ref, qt_ref, k_ref, vt_ref, o_ref, s_ref, m_ref,
                   alpha_ref, acc_ref, *, t, cq, kb, kp, nitems):
    g = pl.program_id(0)
    nchunk = t // cq
    a = jnp.minimum(g, nitems - 1)
    b = jnp.maximum(g - 1, 0)
    qi_a, ki_a = qi_ref[a], ki_ref[a]

    @pl.when(g == 0)
    def _():
        s_ref[...] = jnp.zeros(s_ref.shape, F32)
        m_ref[...] = jnp.zeros(m_ref.shape, F32)
        alpha_ref[...] = jnp.zeros(alpha_ref.shape, F32)
        acc_ref[...] = jnp.zeros(acc_ref.shape, F32)

    def body(diag_a, diag_b):
        for c in range(nchunk):
            m_b = m_ref[c]
            rows_a = (c + 1) * cq if diag_a else t
            rows_b = (c + 1) * cq if diag_b else t
            cmax, pv = None, None
            for i in range(t // kp):
                for j in range(i * kp // kb, min((i + 1) * kp, rows_b) // kb):
                    pt = jnp.exp2(s_ref[c, j * kb:(j + 1) * kb, :] - m_b)
                    d = jnp.dot(vt_ref[:, j * kb:(j + 1) * kb], pt.astype(BF16),
                                preferred_element_type=F32)
                    pv = d if pv is None else pv + d
                if i * kp < rows_a:
                    s = jnp.dot(k_ref[i * kp:(i + 1) * kp, :], qt_ref[:, c * cq:(c + 1) * cq],
                                preferred_element_type=F32)
                    if diag_a and (i + 1) * kp > c * cq:
                        k_pos = i * kp + lax.broadcasted_iota(jnp.int32, (kp, cq), 0)
                        q_pos = c * cq + lax.broadcasted_iota(jnp.int32, (kp, cq), 1)
                        s = jnp.where(k_pos <= q_pos, s, -jnp.inf)
                    s_ref[c, i * kp:(i + 1) * kp, :] = s
                    pmax = jnp.max(s, axis=0, keepdims=True)
                    cmax = pmax if cmax is None else jnp.maximum(cmax, pmax)
            acc_ref[c] = alpha_ref[c] * acc_ref[c] + pv
            m_prev = jnp.where(ki_a == 0, -jnp.inf, m_b)
            m_new = jnp.maximum(m_prev, cmax)
            m_ref[c] = m_new
            alpha_ref[c] = jnp.exp2(m_prev - m_new)

    is_diag_b = ki_ref[b] == qi_ref[b]
    for diag_a in (False, True):
        for diag_b in (False, True):
            @pl.when(((ki_a == qi_a) == diag_a) & (is_diag_b == diag_b))
            def _(diag_a=diag_a, diag_b=diag_b):
                body(diag_a, diag_b)

    @pl.when(is_diag_b)
    def _():
        for c in range(nchunk):
            acc = acc_ref[c]
            o = acc[:B_V] / acc[B_V:B_V + 1]
            o_ref[c * cq:(c + 1) * cq, :] = o.T.astype(o_ref.dtype)


def _b_attn(qt, k, vt):
    s = k.shape[0]
    t, cq = B_ATT_T, B_ATT_CQ
    items = [(h, a, b) for h in range(B_HEADS) for a in range(s // t) for b in range(a + 1)]
    n = len(items)
    hh, qi, ki = (jnp.asarray(np.array(col, np.int32)) for col in zip(*items))
    cur = lambda g: jnp.minimum(g, n - 1)
    prev = lambda g: jnp.maximum(g - 1, 0)
    nchunk = t // cq
    grid_spec = pltpu.PrefetchScalarGridSpec(
        num_scalar_prefetch=3,
        grid=(n + 1,),
        in_specs=[
            pl.BlockSpec((B_QK_PAD, t), lambda g, qi, ki, hh: (hh[cur(g)], qi[cur(g)])),
            pl.BlockSpec((t, B_QK_PAD), lambda g, qi, ki, hh: (ki[cur(g)], hh[cur(g)])),
            pl.BlockSpec((B_VE, t), lambda g, qi, ki, hh: (hh[prev(g)], ki[prev(g)])),
        ],
        out_specs=pl.BlockSpec((t, B_V), lambda g, qi, ki, hh: (qi[prev(g)], hh[prev(g)])),
        scratch_shapes=[
            pltpu.VMEM((nchunk, t, cq), F32),
            pltpu.VMEM((nchunk, 1, cq), F32),
            pltpu.VMEM((nchunk, 1, cq), F32),
            pltpu.VMEM((nchunk, B_VE, cq), F32),
        ],
    )
    return pl.pallas_call(
        functools.partial(_b_attn_kernel, t=t, cq=cq, kb=B_ATT_KB, kp=B_ATT_KP, nitems=n),
        grid_spec=grid_spec,
        out_shape=jax.ShapeDtypeStruct((s, B_HEADS * B_V), BF16),
        compiler_params=_params("arbitrary"),
        name="b_attn",
    )(qi, ki, hh, qt, k, vt)


def _b_wo_kernel(o_ref, h_ref, w_ref, out_ref):
    out_ref[...] = h_ref[...] + jnp.dot(o_ref[...], w_ref[...], preferred_element_type=F32)


def _b_wo(o, h, w_o):
    s = h.shape[0]
    tm = B_WO_TM
    return pl.pallas_call(
        _b_wo_kernel,
        grid=(s // tm,),
        in_specs=[pl.BlockSpec((tm, B_HEADS * B_V), lambda i: (i, 0)),
                  pl.BlockSpec((tm, D_MODEL), lambda i: (i, 0)),
                  pl.BlockSpec((B_HEADS * B_V, D_MODEL), lambda i: (0, 0))],
        out_specs=pl.BlockSpec((tm, D_MODEL), lambda i: (i, 0)),
        out_shape=jax.ShapeDtypeStruct((s, D_MODEL), F32),
        compiler_params=_params("parallel"),
        name="b_wo",
    )(o, h, w_o)


def _rope_freq(dim):
    return ROPE_THETA ** (-jnp.arange(0, dim, 2, dtype=F32) / dim)


def _lane_freq(dim):
    f = _rope_freq(dim)
    return jnp.concatenate([f, f, jnp.zeros((LANES - dim,), F32)])


def _angle_parts(seq_len, tile, offsets, freq):
    start = (jnp.arange(seq_len // tile, dtype=F32) * tile)[:, None] * freq[None, :]
    off = jnp.asarray(offsets, F32)[:, None] * freq[None, :]
    return jnp.cos(start), jnp.sin(start), jnp.cos(off), jnp.sin(off)


def _rope_tables(seq_len, tile, offsets, freq):
    ca, sa, cb, sb = _angle_parts(seq_len, tile, offsets, freq)
    cos = ca[:, None, :] * cb[None] - sa[:, None, :] * sb[None]
    sin = sa[:, None, :] * cb[None] + ca[:, None, :] * sb[None]
    return cos.reshape(seq_len, -1), sin.reshape(seq_len, -1)


def _rope_tables_t(seq_len, tile, freq):
    ca, sa, cb, sb = (t.T for t in _angle_parts(seq_len, tile, np.arange(tile), freq))
    cos = ca[:, :, None] * cb[:, None, :] - sa[:, :, None] * sb[:, None, :]
    sin = sa[:, :, None] * cb[:, None, :] + ca[:, :, None] * sb[:, None, :]
    return cos.reshape(-1, seq_len), sin.reshape(-1, seq_len)


def kernel(x, attn_norm_g, ffn_norm_g, a_w_qkv, a_w_o, kv_norm_g, b_w_kv_a, b_kv_a_norm_g, b_w_kv_b,
           b_w_q_a, b_q_a_norm_g, b_w_q_b, b_w_o, ffn_w_gu, ffn_w_down, final_norm_g):
    assert x.shape[0] == 1 and x.shape[2] == D_MODEL
    s = x.shape[1]
    assert s % QKV_TM == 0 and s % (A_GROUPS[-1][1] * A_ATT_TQ) == 0 and s % B_ATT_T == 0
    h = x[0]
    row = lambda g: g.reshape(1, -1).astype(F32)
    tabs_b = _rope_tables(s, B_PROJ_TM, np.arange(B_PROJ_TM), _lane_freq(B_ROPE))
    tabs_bt = _rope_tables_t(s, B_PROJ_TM, _rope_freq(B_ROPE))

    w_qkv = a_w_qkv[0].astype(BF16)
    outs, lses = [], []
    for group, (_, d) in enumerate(A_GROUPS):
        qkv = _a_qkv(h, row(attn_norm_g[0]), w_qkv, group, d)
        o, lse = _a_attn(qkv, d)
        outs.append(o)
        lses.append(lse)
    h = _a_wo(outs, lses, h, a_w_o[0].astype(BF16))
    w_gu, w_down = ffn_w_gu.astype(BF16), ffn_w_down.astype(BF16)
    h = _ffn(h, row(ffn_norm_g[0]), w_gu, w_down, 0, row(final_norm_g), final_norm=False)

    w_kv_a = jnp.pad(b_w_kv_a, ((0, 0), (0, LANES - B_ROPE))).astype(BF16)
    qt, k, vt = _b_proj(h, row(attn_norm_g[1]), row(kv_norm_g), b_w_q_a[0].astype(BF16), w_kv_a,
                        row(b_q_a_norm_g[0]), row(b_kv_a_norm_g), b_w_q_b[0].T.astype(BF16),
                        b_w_kv_b.astype(BF16), tabs_b, tabs_bt)
    o = _b_attn(qt, k, vt)
    h = _b_wo(o, h, b_w_o[0].astype(BF16))
    h = _ffn(h, row(ffn_norm_g[1]), w_gu, w_down, 1, row(final_norm_g), final_norm=True)
    return h[None]
```

```python
import functools
import math

import jax
import jax.numpy as jnp
import numpy as np
from jax import lax
from jax.experimental import pallas as pl
from jax.experimental.pallas import tpu as pltpu

F32 = jnp.float32
BF16 = jnp.bfloat16

D_MODEL = 2048
ROPE_THETA = 500000.0
NORM_EPS = 1e-6
LANES = 128

A_HEAD_DIM = 128
A_HEADS = D_MODEL // A_HEAD_DIM
A_ROT_DIM = A_HEAD_DIM // 4
A_GROUPS = ((128, 1), (512, 4), (2048, 16))
A_WIDTH = A_HEADS * A_HEAD_DIM
A_SPAN_BLK = 128

B_HEADS = D_MODEL // 128
B_NOPE = 128
B_ROPE = 64
B_V = 128
B_VE = B_V + 16
B_KV_LORA = 512
B_QK_PAD = 256

FFN_HIDDEN = 5632

VMEM_LIMIT = 56 * 1024 * 1024

QKV_TM, QKV_TN = 1024, 1024
QKV_XSLABS = 4
A_ATT_TQ = 512
A_WO_TM = 256
A_WO_NCOL = 4
FFN_TM, FFN_TF = 512, 512
B_PROJ_TM = 256
B_ATT_T, B_ATT_CQ = 2048, 512
B_ATT_KB = 256
B_ATT_KP = 512
B_WO_TM = 512


def _params(*sem):
    return pltpu.CompilerParams(dimension_semantics=sem, vmem_limit_bytes=VMEM_LIMIT)


def _rms_scale(x):
    return x * lax.rsqrt(jnp.mean(x * x, axis=-1, keepdims=True) + NORM_EPS)


def _split_sin(s, half):
    lane = lax.broadcasted_iota(jnp.int32, s.shape, 1)
    return jnp.where(lane < half, -s, 0.0), jnp.where(lane >= half, s, 0.0)


def _rope_tile(a, c, s_lo, s_hi, half):
    return (a * c + pltpu.roll(a, LANES - half, 1) * s_lo + pltpu.roll(a, half, 1) * s_hi)


def _a_qkv_kernel(x_ref, g_ref, w_ref, c_ref, s_ref, o_ref, xn_ref, acc_ref, *xs,
                  d, tm, tn, ncol, nsteps):
    g = pl.program_id(0)
    sub = tm // d

    @pl.when(g == 0)
    def _():
        acc_ref[...] = jnp.zeros(acc_ref.shape, F32)

    @pl.when((g % ncol == 0) & (g < nsteps - 1))
    def _():
        if d == 1:
            xn_ref[...] = (_rms_scale(x_ref[...]) * g_ref[...]).astype(BF16)
        else:
            x = x_ref[...]
            inv = lax.rsqrt(jnp.mean(x * x, axis=-1, keepdims=True) + NORM_EPS)
            nslab = xs[0].shape[0]
            for base in range(0, D_MODEL // LANES, nslab):
                for k in range(nslab):
                    cols = slice((base + k) * LANES, (base + k + 1) * LANES)
                    xs[0][k] = x_ref[:, cols] * inv * g_ref[:, cols]
                for k in range(nslab):
                    for r in range(d):
                        xn_ref[r * sub:(r + 1) * sub, (base + k) * LANES:(base + k + 1) * LANES] = (
                            xs[0][k, pl.ds(r, sub, stride=d), :].astype(BF16))

    t = (jnp.maximum(g - 1, 0) % ncol) // (A_WIDTH // tn)
    rot = t < 2
    qscale = jnp.where(t == 0, A_HEAD_DIM ** -0.5 * math.log2(math.e), 1.0).astype(F32)
    c = jnp.where(rot, c_ref[...], 1.0) * qscale
    s_lo, s_hi = _split_sin(jnp.where(rot, s_ref[...], 0.0) * qscale, A_ROT_DIM // 2)
    for col in range(tn // LANES):
        sl = slice(col * LANES, (col + 1) * LANES)
        a = _rope_tile(acc_ref[:, sl], c, s_lo, s_hi, A_ROT_DIM // 2).astype(BF16)
        for r in range(d):
            o_ref[r, :, sl] = a[r * sub:(r + 1) * sub]

    acc_ref[...] = jnp.dot(xn_ref[...], w_ref[...], preferred_element_type=F32)


def _a_qkv(x, gain, w_qkv, group, d):
    s = x.shape[0]
    tm, tn = QKV_TM, QKV_TN
    ncol = 3 * A_WIDTH // tn
    nsteps = (s // tm) * ncol + 1
    cur = lambda g: jnp.minimum(g, nsteps - 2)
    prev = lambda g: jnp.maximum(g - 1, 0)
    tab_spec = pl.BlockSpec((tm, LANES), lambda g: (prev(g) // ncol, 0))
    offsets = np.arange(tm).reshape(tm // d, d).T.reshape(-1)
    tabs = _rope_tables(s, tm, offsets, _lane_freq(A_ROT_DIM))
    scratch = [pltpu.VMEM((tm, D_MODEL), BF16), pltpu.VMEM((tm, tn), F32)]
    if d > 1:
        scratch.append(pltpu.VMEM((QKV_XSLABS, tm, LANES), F32))
    return pl.pallas_call(
        functools.partial(_a_qkv_kernel, d=d, tm=tm, tn=tn, ncol=ncol, nsteps=nsteps),
        grid=(nsteps,),
        in_specs=[
            pl.BlockSpec((tm, D_MODEL), lambda g: (cur(g) // ncol, 0)),
            pl.BlockSpec((1, D_MODEL), lambda g: (0, 0)),
            pl.BlockSpec((D_MODEL, tn), lambda g: (0, group * ncol + cur(g) % ncol)),
            tab_spec, tab_spec,
        ],
        out_specs=pl.BlockSpec((d, tm // d, tn), lambda g: (0, prev(g) // ncol, prev(g) % ncol)),
        out_shape=jax.ShapeDtypeStruct((d, s // d, 3 * A_WIDTH), BF16),
        scratch_shapes=scratch,
        compiler_params=_params("arbitrary"),
        name=f"a_qkv_d{d}",
    )(x, gain, w_qkv, *tabs)


def _a_attn_kernel(q_ref, kp_ref, kc_ref, vp_ref, vc_ref, o_ref, m_ref, l_ref, *, tq):
    n = pl.program_id(1)
    blk = A_SPAN_BLK
    qi = lax.broadcasted_iota(jnp.int32, (blk, 2 * blk), 0)
    kj = lax.broadcasted_iota(jnp.int32, (blk, 2 * blk), 1)
    band = (kj >= qi) & (kj <= qi + blk)
    band_first = band & ((kj >= blk) | (n > 0))
    bias = jnp.where(band, 0.0, -jnp.inf)
    bias_first = jnp.where(band_first, 0.0, -jnp.inf)
    lane = lax.broadcasted_iota(jnp.int32, (blk, LANES), 1)
    ones = jnp.ones((2 * blk, LANES), BF16)
    nt = (((1,), (1,)), ((), ()))
    for b in range(tq // blk):
        rows = slice(b * blk, (b + 1) * blk)
        m_tile = jnp.zeros((blk, LANES), F32)
        l_tile = jnp.ones((blk, LANES), F32)
        for h in range(A_HEADS):
            sl = slice(h * A_HEAD_DIM, (h + 1) * A_HEAD_DIM)
            if b == 0:
                k = jnp.concatenate([kp_ref[:, sl], kc_ref[0:blk, sl]], axis=0)
                v = jnp.concatenate([vp_ref[:, sl], vc_ref[0:blk, sl]], axis=0)
            else:
                k = kc_ref[(b - 1) * blk:(b + 1) * blk, sl]
                v = vc_ref[(b - 1) * blk:(b + 1) * blk, sl]
            s = lax.dot_general(q_ref[rows, sl], k, nt, preferred_element_type=F32)
            s = s + (bias_first if b == 0 else bias)
            m = jnp.max(jnp.maximum(s[:, :blk], s[:, blk:]), axis=-1, keepdims=True)
            p = jnp.exp2(s - m).astype(BF16)
            oe = jnp.dot(p, jnp.concatenate([v, ones], axis=1), preferred_element_type=F32)
            l = oe[:, A_HEAD_DIM:]
            o_ref[rows, sl] = (oe[:, :A_HEAD_DIM] / l).astype(o_ref.dtype)
            m_tile = jnp.where(lane == h, m, m_tile)
            l_tile = jnp.where(lane == h, l, l_tile)
        m_ref[rows, :] = m_tile
        l_ref[rows, :] = l_tile


def _a_attn(qkv, d):
    _, m, _ = qkv.shape
    tq = A_ATT_TQ
    ratio = tq // A_SPAN_BLK
    cur = lambda col: pl.BlockSpec((None, tq, A_WIDTH), lambda r, n: (r, n, col))
    prev = lambda col: pl.BlockSpec(
        (None, A_SPAN_BLK, A_WIDTH), lambda r, n: (r, jnp.maximum(n * ratio - 1, 0), col))
    return pl.pallas_call(
        functools.partial(_a_attn_kernel, tq=tq),
        grid=(d, m // tq),
        in_specs=[cur(0), prev(1), cur(1), prev(2), cur(2)],
        out_specs=[
            pl.BlockSpec((None, tq, A_WIDTH), lambda r, n: (r, n, 0)),
            pl.BlockSpec((None, tq, LANES), lambda r, n: (r, n, 0)),
            pl.BlockSpec((None, tq, LANES), lambda r, n: (r, n, 0)),
        ],
        out_shape=[
            jax.ShapeDtypeStruct((d, m, A_WIDTH), BF16),
            jax.ShapeDtypeStruct((d, m, LANES), F32),
            jax.ShapeDtypeStruct((d, m, LANES), F32),
        ],
        compiler_params=_params("parallel", "arbitrary"),
        name=f"a_attn_d{d}",
    )(qkv, qkv, qkv, qkv, qkv)


def _a_wo_kernel(o1_ref, o2_ref, o3_ref, m1_ref, m2_ref, m3_ref, l1_ref, l2_ref, l3_ref, h_ref,
                 w_ref, out_ref, nat2_ref, nat3_ref, mn2_ref, mn3_ref, ln2_ref, ln3_ref,
                 xb0_ref, xb1_ref, *, tm, ncol):
    g = pl.program_id(0)
    heads_per_piece = A_HEADS // ncol
    tn = D_MODEL // ncol

    @pl.when(g == 0)
    def _():
        xb1_ref[...] = jnp.zeros(xb1_ref.shape, BF16)

    def body(xb_new, xb_old):
        for o_ref, m_ref, l_ref, nat_ref, mn_ref, ln_ref, (_, d) in (
                (o2_ref, m2_ref, l2_ref, nat2_ref, mn2_ref, ln2_ref, A_GROUPS[1]),
                (o3_ref, m3_ref, l3_ref, nat3_ref, mn3_ref, ln3_ref, A_GROUPS[2])):
            for r in range(d):
                rows = pl.ds(r, tm // d, stride=d)
                mn_ref[0, rows, :] = m_ref[r]
                ln_ref[0, rows, :] = l_ref[r]
                for h in range(A_HEADS):
                    nat_ref[h, rows, :] = o_ref[r, :, h * A_HEAD_DIM:(h + 1) * A_HEAD_DIM].astype(F32)
        m1, m2, m3 = m1_ref[0], mn2_ref[0], mn3_ref[0]
        mx = jnp.maximum(jnp.maximum(m1, m2), m3)
        e1 = jnp.exp2(m1 - mx) * l1_ref[0]
        e2 = jnp.exp2(m2 - mx) * ln2_ref[0]
        e3 = jnp.exp2(m3 - mx) * ln3_ref[0]
        den = e1 + e2 + e3
        w2, w3 = e2 / den, e3 / den
        for piece in range(ncol):
            for h in range(piece * heads_per_piece, (piece + 1) * heads_per_piece):
                sl = slice(h * A_HEAD_DIM, (h + 1) * A_HEAD_DIM)
                o1 = o1_ref[0, :, sl].astype(F32)
                merged = (o1 + w2[:, h:h + 1] * (nat2_ref[h] - o1)
                          + w3[:, h:h + 1] * (nat3_ref[h] - o1))
                xb_new[:, sl] = merged.astype(BF16)
            cols = slice(piece * tn, (piece + 1) * tn)
            out_ref[:, cols] = h_ref[:, cols] + jnp.dot(xb_old[...], w_ref[:, cols],
                                                         preferred_element_type=F32)

    for parity, bufs in enumerate(((xb0_ref, xb1_ref), (xb1_ref, xb0_ref))):
        @pl.when(g % 2 == parity)
        def _(bufs=bufs):
            body(*bufs)


def _a_wo(outs, maxes, dens, h, w_o):
    s = h.shape[0]
    tm = A_WO_TM
    ntiles = s // tm
    cur = lambda g: jnp.minimum(g, ntiles - 1)
    prev = lambda g: jnp.maximum(g - 1, 0)
    o_spec = lambda d: pl.BlockSpec((d, tm // d, A_WIDTH), lambda g: (0, cur(g), 0))
    l_spec = lambda d: pl.BlockSpec((d, tm // d, LANES), lambda g: (0, cur(g), 0))
    ds_ = [d for _, d in A_GROUPS]
    return pl.pallas_call(
        functools.partial(_a_wo_kernel, tm=tm, ncol=A_WO_NCOL),
        grid=(ntiles + 1,),
        in_specs=[o_spec(ds_[0]), o_spec(ds_[1]), o_spec(ds_[2]),
                  l_spec(ds_[0]), l_spec(ds_[1]), l_spec(ds_[2]),
                  l_spec(ds_[0]), l_spec(ds_[1]), l_spec(ds_[2]),
                  pl.BlockSpec((tm, D_MODEL), lambda g: (prev(g), 0)),
                  pl.BlockSpec((A_WIDTH, D_MODEL), lambda g: (0, 0))],
        out_specs=pl.BlockSpec((tm, D_MODEL), lambda g: (prev(g), 0)),
        out_shape=jax.ShapeDtypeStruct((s, D_MODEL), F32),
        scratch_shapes=[
            pltpu.VMEM((A_HEADS, tm, LANES), F32),
            pltpu.VMEM((A_HEADS, tm, LANES), F32),
            pltpu.VMEM((1, tm, LANES), F32),
            pltpu.VMEM((1, tm, LANES), F32),
            pltpu.VMEM((1, tm, LANES), F32),
            pltpu.VMEM((1, tm, LANES), F32),
            pltpu.VMEM((tm, A_WIDTH), BF16),
            pltpu.VMEM((tm, A_WIDTH), BF16),
        ],
        compiler_params=_params("arbitrary"),
        name="a_wo",
    )(*outs, *maxes, *dens, h, w_o)


def _ffn_kernel(h_ref, g_ref, wg_ref, wu_ref, wd_ref, fg_ref, out_ref, xn_ref, *, final_norm):
    f = pl.program_id(1)

    @pl.when(f == 0)
    def _():
        xn_ref[...] = (_rms_scale(h_ref[...]) * g_ref[...]).astype(BF16)
        out_ref[...] = jnp.zeros(out_ref.shape, F32)

    xn = xn_ref[...]
    gate = jnp.dot(xn, wg_ref[...], preferred_element_type=F32)
    up = jnp.dot(xn, wu_ref[...], preferred_element_type=F32)
    act = (gate * jax.nn.sigmoid(gate) * up).astype(BF16)
    out_ref[...] += jnp.dot(act, wd_ref[...], preferred_element_type=F32)

    @pl.when(f == pl.num_programs(1) - 1)
    def _():
        y = h_ref[...] + out_ref[...]
        if final_norm:
            y = _rms_scale(y) * fg_ref[...]
        out_ref[...] = y


def _ffn(h, gain, w_gu, w_down, layer, final_gain, final_norm):
    s = h.shape[0]
    tm, tf = FFN_TM, FFN_TF
    nf = FFN_HIDDEN // tf
    return pl.pallas_call(
        functools.partial(_ffn_kernel, final_norm=final_norm),
        grid=(s // tm, nf),
        in_specs=[
            pl.BlockSpec((tm, D_MODEL), lambda i, f: (i, 0)),
            pl.BlockSpec((1, D_MODEL), lambda i, f: (0, 0)),
            pl.BlockSpec((None, D_MODEL, tf), lambda i, f: (layer, 0, f)),
            pl.BlockSpec((None, D_MODEL, tf), lambda i, f: (layer, 0, nf + f)),
            pl.BlockSpec((None, tf, D_MODEL), lambda i, f: (layer, f, 0)),
            pl.BlockSpec((1, D_MODEL), lambda i, f: (0, 0)),
        ],
        out_specs=pl.BlockSpec((tm, D_MODEL), lambda i, f: (i, 0)),
        out_shape=jax.ShapeDtypeStruct((s, D_MODEL), F32),
        scratch_shapes=[pltpu.VMEM((tm, D_MODEL), BF16)],
        compiler_params=_params("parallel", "arbitrary"),
        name="ffn_final" if final_norm else "ffn",
    )(h, gain, w_gu, w_gu, w_down, final_gain)


def _b_proj_kernel(h_ref, ga_ref, gk_ref, wqa_ref, wkva_ref, gqa_ref, gkva_ref, wqb_ref, wkvb_ref,
                   c_ref, s_ref, ct_ref, st_ref, qt_ref, k_ref, vt_ref):
    y = _rms_scale(h_ref[...])
    xa = (y * ga_ref[...]).astype(BF16)
    xk = (y * gk_ref[...]).astype(BF16)
    c = c_ref[...]
    s_lo, s_hi = _split_sin(s_ref[...], B_ROPE // 2)

    cq = jnp.dot(xa, wqa_ref[...], preferred_element_type=F32)
    cq = (_rms_scale(cq) * gqa_ref[...]).astype(BF16)
    qt = lax.dot_general(wqb_ref[...], cq, (((1,), (1,)), ((), ())),
                         preferred_element_type=F32)
    qscale = (B_NOPE + B_ROPE) ** -0.5 * math.log2(math.e)
    cos_t, sin_t = ct_ref[...], st_ref[...]
    half = B_ROPE // 2
    for h in range(B_HEADS):
        lo = h * B_QK_PAD
        src = h * (B_NOPE + B_ROPE)
        qt_ref[lo:lo + B_NOPE, :] = (qt[src:src + B_NOPE] * qscale).astype(BF16)
        x1 = qt[src + B_NOPE:src + B_NOPE + half]
        x2 = qt[src + B_NOPE + half:src + B_NOPE + B_ROPE]
        qt_ref[lo + B_NOPE:lo + B_NOPE + half, :] = (
            (x1 * cos_t - x2 * sin_t) * qscale).astype(BF16)
        qt_ref[lo + B_NOPE + half:lo + B_NOPE + B_ROPE, :] = (
            (x2 * cos_t + x1 * sin_t) * qscale).astype(BF16)
        qt_ref[lo + B_NOPE + B_ROPE:lo + B_QK_PAD, :] = jnp.zeros(
            (B_QK_PAD - B_NOPE - B_ROPE, qt.shape[1]), BF16)

    ckv = jnp.dot(xk, wkva_ref[...], preferred_element_type=F32)
    lat = (_rms_scale(ckv[:, :B_KV_LORA]) * gkva_ref[...]).astype(BF16)
    k_pe = _rope_tile(ckv[:, B_KV_LORA:], c, s_lo, s_hi, B_ROPE // 2).astype(BF16)
    kv = jnp.dot(lat, wkvb_ref[...], preferred_element_type=F32)
    for h in range(B_HEADS):
        lo = h * B_QK_PAD
        k_ref[:, lo:lo + B_NOPE] = kv[:, lo:lo + B_NOPE].astype(BF16)
        k_ref[:, lo + B_NOPE:lo + B_QK_PAD] = k_pe
        vt_ref[h * B_VE:h * B_VE + B_V, :] = kv[:, lo + B_NOPE:lo + B_QK_PAD].T.astype(BF16)
        vt_ref[h * B_VE + B_V:(h + 1) * B_VE, :] = jnp.ones((B_VE - B_V, h_ref.shape[0]), BF16)


def _b_proj(h, g_attn, g_kv, w_q_a, w_kv_a, g_qa, g_kva, w_q_b_t, w_kv_b, tabs, tabs_t):
    s = h.shape[0]
    tm = B_PROJ_TM
    full = lambda a: pl.BlockSpec(a.shape, lambda i: (0,) * a.ndim)
    row = lambda width: pl.BlockSpec((tm, width), lambda i: (i, 0))
    col = lambda height: pl.BlockSpec((height, tm), lambda i: (0, i))
    return pl.pallas_call(
        _b_proj_kernel,
        grid=(s // tm,),
        in_specs=[row(D_MODEL), full(g_attn), full(g_kv), full(w_q_a), full(w_kv_a), full(g_qa),
                  full(g_kva), full(w_q_b_t), full(w_kv_b), row(LANES), row(LANES),
                  col(B_ROPE // 2), col(B_ROPE // 2)],
        out_specs=[col(B_HEADS * B_QK_PAD), row(B_HEADS * B_QK_PAD), col(B_HEADS * B_VE)],
        out_shape=[
            jax.ShapeDtypeStruct((B_HEADS * B_QK_PAD, s), BF16),
            jax.ShapeDtypeStruct((s, B_HEADS * B_QK_PAD), BF16),
            jax.ShapeDtypeStruct((B_HEADS * B_VE, s), BF16),
        ],
        compiler_params=_params("parallel"),
        name="b_proj",
    )(h, g_attn, g_kv, w_q_a, w_kv_a, g_qa, g_kva, w_q_b_t, w_kv_b, *tabs, *tabs_t)


def _b_attn_kernel(qi_ref, ki_ref, hh_ref, qt_ref, k_ref, vt_ref, o_ref, s_ref, m_ref,
                   alpha_ref, acc_ref, *, t, cq, kb, kp, nitems):
    g = pl.program_id(0)
    nchunk = t // cq
    a = jnp.minimum(g, nitems - 1)
    b = jnp.maximum(g - 1, 0)
    qi_a, ki_a = qi_ref[a], ki_ref[a]

    @pl.when(g == 0)
    def _():
        s_ref[...] = jnp.zeros(s_ref.shape, F32)
        m_ref[...] = jnp.zeros(m_ref.shape, F32)
        alpha_ref[...] = jnp.zeros(alpha_ref.shape, F32)
        acc_ref[...] = jnp.zeros(acc_ref.shape, F32)

    def body(diag_a, diag_b):
        for c in range(nchunk):
            m_b = m_ref[c]
            rows_a = (c + 1) * cq if diag_a else t
            rows_b = (c + 1) * cq if diag_b else t
            cmax, pv = None, None
            for i in range(t // kp):
                for j in range(i * kp // kb, min((i + 1) * kp, rows_b) // kb):
                    pt = jnp.exp2(s_ref[c, j * kb:(j + 1) * kb, :] - m_b)
                    d = jnp.dot(vt_ref[:, j * kb:(j + 1) * kb], pt.astype(BF16),
                                preferred_element_type=F32)
                    pv = d if pv is None else pv + d
                if i * kp < rows_a:
                    s = jnp.dot(k_ref[i * kp:(i + 1) * kp, :], qt_ref[:, c * cq:(c + 1) * cq],
                                preferred_element_type=F32)
                    if diag_a and (i + 1) * kp > c * cq:
                        k_pos = i * kp + lax.broadcasted_iota(jnp.int32, (kp, cq), 0)
                        q_pos = c * cq + lax.broadcasted_iota(jnp.int32, (kp, cq), 1)
                        s = jnp.where(k_pos <= q_pos, s, -jnp.inf)
                    s_ref[c, i * kp:(i + 1) * kp, :] = s
                    pmax = jnp.max(s, axis=0, keepdims=True)
                    cmax = pmax if cmax is None else jnp.maximum(cmax, pmax)
            acc_ref[c] = alpha_ref[c] * acc_ref[c] + pv
            m_prev = jnp.where(ki_a == 0, -jnp.inf, m_b)
            m_new = jnp.maximum(m_prev, cmax)
            m_ref[c] = m_new
            alpha_ref[c] = jnp.exp2(m_prev - m_new)

    is_diag_b = ki_ref[b] == qi_ref[b]
    for diag_a in (False, True):
        for diag_b in (False, True):
            @pl.when(((ki_a == qi_a) == diag_a) & (is_diag_b == diag_b))
            def _(diag_a=diag_a, diag_b=diag_b):
                body(diag_a, diag_b)

    @pl.when(is_diag_b)
    def _():
        for c in range(nchunk):
            acc = acc_ref[c]
            o = acc[:B_V] / acc[B_V:B_V + 1]
            o_ref[c * cq:(c + 1) * cq, :] = o.T.astype(o_ref.dtype)


def _b_attn(qt, k, vt):
    s = k.shape[0]
    t, cq = B_ATT_T, B_ATT_CQ
    items = [(h, a, b) for h in range(B_HEADS) for a in range(s // t) for b in range(a + 1)]
    n = len(items)
    hh, qi, ki = (jnp.asarray(np.array(col, np.int32)) for col in zip(*items))
    cur = lambda g: jnp.minimum(g, n - 1)
    prev = lambda g: jnp.maximum(g - 1, 0)
    nchunk = t // cq
    grid_spec = pltpu.PrefetchScalarGridSpec(
        num_scalar_prefetch=3,
        grid=(n + 1,),
        in_specs=[
            pl.BlockSpec((B_QK_PAD, t), lambda g, qi, ki, hh: (hh[cur(g)], qi[cur(g)])),
            pl.BlockSpec((t, B_QK_PAD), lambda g, qi, ki, hh: (ki[cur(g)], hh[cur(g)])),
            pl.BlockSpec((B_VE, t), lambda g, qi, ki, hh: (hh[prev(g)], ki[prev(g)])),
        ],
        out_specs=pl.BlockSpec((t, B_V), lambda g, qi, ki, hh: (qi[prev(g)], hh[prev(g)])),
        scratch_shapes=[
            pltpu.VMEM((nchunk, t, cq), F32),
            pltpu.VMEM((nchunk, 1, cq), F32),
            pltpu.VMEM((nchunk, 1, cq), F32),
            pltpu.VMEM((nchunk, B_VE, cq), F32),
        ],
    )
    return pl.pallas_call(
        functools.partial(_b_attn_kernel, t=t, cq=cq, kb=B_ATT_KB, kp=B_ATT_KP, nitems=n),
        grid_spec=grid_spec,
        out_shape=jax.ShapeDtypeStruct((s, B_HEADS * B_V), BF16),
        compiler_params=_params("arbitrary"),
        name="b_attn",
    )(qi, ki, hh, qt, k, vt)


def _b_wo_kernel(o_ref, h_ref, w_ref, out_ref):
    out_ref[...] = h_ref[...] + jnp.dot(o_ref[...], w_ref[...], preferred_element_type=F32)


def _b_wo(o, h, w_o):
    s = h.shape[0]
    tm = B_WO_TM
    return pl.pallas_call(
        _b_wo_kernel,
        grid=(s // tm,),
        in_specs=[pl.BlockSpec((tm, B_HEADS * B_V), lambda i: (i, 0)),
                  pl.BlockSpec((tm, D_MODEL), lambda i: (i, 0)),
                  pl.BlockSpec((B_HEADS * B_V, D_MODEL), lambda i: (0, 0))],
        out_specs=pl.BlockSpec((tm, D_MODEL), lambda i: (i, 0)),
        out_shape=jax.ShapeDtypeStruct((s, D_MODEL), F32),
        compiler_params=_params("parallel"),
        name="b_wo",
    )(o, h, w_o)


def _rope_freq(dim):
    return ROPE_THETA ** (-jnp.arange(0, dim, 2, dtype=F32) / dim)


def _lane_freq(dim):
    f = _rope_freq(dim)
    return jnp.concatenate([f, f, jnp.zeros((LANES - dim,), F32)])


def _angle_parts(seq_len, tile, offsets, freq):
    start = (jnp.arange(seq_len // tile, dtype=F32) * tile)[:, None] * freq[None, :]
    off = jnp.asarray(offsets, F32)[:, None] * freq[None, :]
    return jnp.cos(start), jnp.sin(start), jnp.cos(off), jnp.sin(off)


def _rope_tables(seq_len, tile, offsets, freq):
    ca, sa, cb, sb = _angle_parts(seq_len, tile, offsets, freq)
    cos = ca[:, None, :] * cb[None] - sa[:, None, :] * sb[None]
    sin = sa[:, None, :] * cb[None] + ca[:, None, :] * sb[None]
    return cos.reshape(seq_len, -1), sin.reshape(seq_len, -1)


def _rope_tables_t(seq_len, tile, freq):
    ca, sa, cb, sb = (t.T for t in _angle_parts(seq_len, tile, np.arange(tile), freq))
    cos = ca[:, :, None] * cb[:, None, :] - sa[:, :, None] * sb[:, None, :]
    sin = sa[:, :, None] * cb[:, None, :] + ca[:, :, None] * sb[:, None, :]
    return cos.reshape(-1, seq_len), sin.reshape(-1, seq_len)


def kernel(x, attn_norm_g, ffn_norm_g, a_w_qkv, a_w_o, kv_norm_g, b_w_kv_a, b_kv_a_norm_g, b_w_kv_b,
           b_w_q_a, b_q_a_norm_g, b_w_q_b, b_w_o, ffn_w_gu, ffn_w_down, final_norm_g):
    assert x.shape[0] == 1 and x.shape[2] == D_MODEL
    s = x.shape[1]
    assert s % QKV_TM == 0 and s % (A_GROUPS[-1][1] * A_ATT_TQ) == 0 and s % B_ATT_T == 0
    h = x[0]
    row = lambda g: g.reshape(1, -1).astype(F32)
    tabs_b = _rope_tables(s, B_PROJ_TM, np.arange(B_PROJ_TM), _lane_freq(B_ROPE))
    tabs_bt = _rope_tables_t(s, B_PROJ_TM, _rope_freq(B_ROPE))

    w_qkv = a_w_qkv[0].astype(BF16)
    outs, maxes, dens = [], [], []
    for group, (_, d) in enumerate(A_GROUPS):
        qkv = _a_qkv(h, row(attn_norm_g[0]), w_qkv, group, d)
        o, row_max, den = _a_attn(qkv, d)
        outs.append(o)
        maxes.append(row_max)
        dens.append(den)
    h = _a_wo(outs, maxes, dens, h, a_w_o[0].astype(BF16))
    w_gu, w_down = ffn_w_gu.astype(BF16), ffn_w_down.astype(BF16)
    h = _ffn(h, row(ffn_norm_g[0]), w_gu, w_down, 0, row(final_norm_g), final_norm=False)

    w_kv_a = jnp.pad(b_w_kv_a, ((0, 0), (0, LANES - B_ROPE))).astype(BF16)
    qt, k, vt = _b_proj(h, row(attn_norm_g[1]), row(kv_norm_g), b_w_q_a[0].astype(BF16), w_kv_a,
                        row(b_q_a_norm_g[0]), row(b_kv_a_norm_g), b_w_q_b[0].T.astype(BF16),
                        b_w_kv_b.astype(BF16), tabs_b, tabs_bt)
    o = _b_attn(qt, k, vt)
    h = _b_wo(o, h, b_w_o[0].astype(BF16))
    h = _ffn(h, row(ffn_norm_g[1]), w_gu, w_down, 1, row(final_norm_g), final_norm=True)
    return h[None]
```

```python
import functools
import math

import jax
import jax.numpy as jnp
import numpy as np
from jax import lax
from jax.experimental import pallas as pl
from jax.experimental.pallas import tpu as pltpu

F32 = jnp.float32
BF16 = jnp.bfloat16

D_MODEL = 2048
ROPE_THETA = 500000.0
NORM_EPS = 1e-6
LANES = 128

A_HEAD_DIM = 128
A_HEADS = D_MODEL // A_HEAD_DIM
A_ROT_DIM = A_HEAD_DIM // 4
A_GROUPS = ((128, 1), (512, 4), (2048, 16))
A_WIDTH = A_HEADS * A_HEAD_DIM
A_SPAN_BLK = 128

B_HEADS = D_MODEL // 128
B_NOPE = 128
B_ROPE = 64
B_V = 128
B_VE = B_V + 16
B_KV_LORA = 512
B_QK_PAD = 256

FFN_HIDDEN = 5632

VMEM_LIMIT = 56 * 1024 * 1024

QKV_TM, QKV_TN = 1024, 1024
QKV_XSLABS = 4
A_ATT_TQ = 512
A_WO_TM = 256
A_WO_NCOL = 4
FFN_TM, FFN_TF = 512, 512
B_PROJ_TM = 256
B_ATT_T, B_ATT_CQ = 2048, 512
B_ATT_HP = 2
B_ATT_KB = 256
B_ATT_KP = 512
B_WO_TM = 512


def _params(*sem):
    return pltpu.CompilerParams(dimension_semantics=sem, vmem_limit_bytes=VMEM_LIMIT)


def _rms_scale(x):
    return x * lax.rsqrt(jnp.mean(x * x, axis=-1, keepdims=True) + NORM_EPS)


def _split_sin(s, half):
    lane = lax.broadcasted_iota(jnp.int32, s.shape, 1)
    return jnp.where(lane < half, -s, 0.0), jnp.where(lane >= half, s, 0.0)


def _rope_tile(a, c, s_lo, s_hi, half):
    return (a * c + pltpu.roll(a, LANES - half, 1) * s_lo + pltpu.roll(a, half, 1) * s_hi)


def _a_qkv_kernel(x_ref, g_ref, w_ref, c_ref, s_ref, o_ref, xn_ref, acc_ref, *xs,
                  d, tm, tn, ncol, nsteps):
    g = pl.program_id(0)
    sub = tm // d

    @pl.when(g == 0)
    def _():
        acc_ref[...] = jnp.zeros(acc_ref.shape, F32)

    @pl.when((g % ncol == 0) & (g < nsteps - 1))
    def _():
        if d == 1:
            xn_ref[...] = (_rms_scale(x_ref[...]) * g_ref[...]).astype(BF16)
        else:
            x = x_ref[...]
            inv = lax.rsqrt(jnp.mean(x * x, axis=-1, keepdims=True) + NORM_EPS)
            nslab = xs[0].shape[0]
            for base in range(0, D_MODEL // LANES, nslab):
                for k in range(nslab):
                    cols = slice((base + k) * LANES, (base + k + 1) * LANES)
                    xs[0][k] = x_ref[:, cols] * inv * g_ref[:, cols]
                for k in range(nslab):
                    for r in range(d):
                        xn_ref[r * sub:(r + 1) * sub, (base + k) * LANES:(base + k + 1) * LANES] = (
                            xs[0][k, pl.ds(r, sub, stride=d), :].astype(BF16))

    t = (jnp.maximum(g - 1, 0) % ncol) // (A_WIDTH // tn)
    rot = t < 2
    qscale = jnp.where(t == 0, A_HEAD_DIM ** -0.5 * math.log2(math.e), 1.0).astype(F32)
    c = jnp.where(rot, c_ref[...], 1.0) * qscale
    s_lo, s_hi = _split_sin(jnp.where(rot, s_ref[...], 0.0) * qscale, A_ROT_DIM // 2)
    for col in range(tn // LANES):
        sl = slice(col * LANES, (col + 1) * LANES)
        a = _rope_tile(acc_ref[:, sl], c, s_lo, s_hi, A_ROT_DIM // 2).astype(BF16)
        for r in range(d):
            o_ref[r, :, sl] = a[r * sub:(r + 1) * sub]

    acc_ref[...] = jnp.dot(xn_ref[...], w_ref[...], preferred_element_type=F32)


def _a_qkv(x, gain, w_qkv, group, d):
    s = x.shape[0]
    tm, tn = QKV_TM, QKV_TN
    ncol = 3 * A_WIDTH // tn
    nsteps = (s // tm) * ncol + 1
    cur = lambda g: jnp.minimum(g, nsteps - 2)
    prev = lambda g: jnp.maximum(g - 1, 0)
    tab_spec = pl.BlockSpec((tm, LANES), lambda g: (prev(g) // ncol, 0))
    offsets = np.arange(tm).reshape(tm // d, d).T.reshape(-1)
    tabs = _rope_tables(s, tm, offsets, _lane_freq(A_ROT_DIM))
    scratch = [pltpu.VMEM((tm, D_MODEL), BF16), pltpu.VMEM((tm, tn), F32)]
    if d > 1:
        scratch.append(pltpu.VMEM((QKV_XSLABS, tm, LANES), F32))
    return pl.pallas_call(
        functools.partial(_a_qkv_kernel, d=d, tm=tm, tn=tn, ncol=ncol, nsteps=nsteps),
        grid=(nsteps,),
        in_specs=[
            pl.BlockSpec((tm, D_MODEL), lambda g: (cur(g) // ncol, 0)),
            pl.BlockSpec((1, D_MODEL), lambda g: (0, 0)),
            pl.BlockSpec((D_MODEL, tn), lambda g: (0, group * ncol + cur(g) % ncol)),
            tab_spec, tab_spec,
        ],
        out_specs=pl.BlockSpec((d, tm // d, tn), lambda g: (0, prev(g) // ncol, prev(g) % ncol)),
        out_shape=jax.ShapeDtypeStruct((d, s // d, 3 * A_WIDTH), BF16),
        scratch_shapes=scratch,
        compiler_params=_params("arbitrary"),
        name=f"a_qkv_d{d}",
    )(x, gain, w_qkv, *tabs)


def _a_attn_kernel(q_ref, kp_ref, kc_ref, vp_ref, vc_ref, o_ref, m_ref, l_ref, *, tq):
    n = pl.program_id(1)
    blk = A_SPAN_BLK
    qi = lax.broadcasted_iota(jnp.int32, (blk, 2 * blk), 0)
    kj = lax.broadcasted_iota(jnp.int32, (blk, 2 * blk), 1)
    band = (kj >= qi) & (kj <= qi + blk)
    band_first = band & ((kj >= blk) | (n > 0))
    bias = jnp.where(band, 0.0, -jnp.inf)
    bias_first = jnp.where(band_first, 0.0, -jnp.inf)
    lane = lax.broadcasted_iota(jnp.int32, (blk, LANES), 1)
    ones = jnp.ones((2 * blk, LANES), BF16)
    nt = (((1,), (1,)), ((), ()))
    for b in range(tq // blk):
        rows = slice(b * blk, (b + 1) * blk)
        m_tile = jnp.zeros((blk, LANES), F32)
        l_tile = jnp.ones((blk, LANES), F32)
        for h in range(A_HEADS):
            sl = slice(h * A_HEAD_DIM, (h + 1) * A_HEAD_DIM)
            if b == 0:
                k = jnp.concatenate([kp_ref[:, sl], kc_ref[0:blk, sl]], axis=0)
                v = jnp.concatenate([vp_ref[:, sl], vc_ref[0:blk, sl]], axis=0)
            else:
                k = kc_ref[(b - 1) * blk:(b + 1) * blk, sl]
                v = vc_ref[(b - 1) * blk:(b + 1) * blk, sl]
            s = lax.dot_general(q_ref[rows, sl], k, nt, preferred_element_type=F32)
            s = s + (bias_first if b == 0 else bias)
            m = jnp.max(jnp.maximum(s[:, :blk], s[:, blk:]), axis=-1, keepdims=True)
            p = jnp.exp2(s - m).astype(BF16)
            oe = jnp.dot(p, jnp.concatenate([v, ones], axis=1), preferred_element_type=F32)
            l = oe[:, A_HEAD_DIM:]
            o_ref[rows, sl] = (oe[:, :A_HEAD_DIM] / l).astype(o_ref.dtype)
            m_tile = jnp.where(lane == h, m, m_tile)
            l_tile = jnp.where(lane == h, l, l_tile)
        m_ref[rows, :] = m_tile
        l_ref[rows, :] = l_tile


def _a_attn(qkv, d):
    _, m, _ = qkv.shape
    tq = A_ATT_TQ
    ratio = tq // A_SPAN_BLK
    cur = lambda col: pl.BlockSpec((None, tq, A_WIDTH), lambda r, n: (r, n, col))
    prev = lambda col: pl.BlockSpec(
        (None, A_SPAN_BLK, A_WIDTH), lambda r, n: (r, jnp.maximum(n * ratio - 1, 0), col))
    return pl.pallas_call(
        functools.partial(_a_attn_kernel, tq=tq),
        grid=(d, m // tq),
        in_specs=[cur(0), prev(1), cur(1), prev(2), cur(2)],
        out_specs=[
            pl.BlockSpec((None, tq, A_WIDTH), lambda r, n: (r, n, 0)),
            pl.BlockSpec((None, tq, LANES), lambda r, n: (r, n, 0)),
            pl.BlockSpec((None, tq, LANES), lambda r, n: (r, n, 0)),
        ],
        out_shape=[
            jax.ShapeDtypeStruct((d, m, A_WIDTH), BF16),
            jax.ShapeDtypeStruct((d, m, LANES), F32),
            jax.ShapeDtypeStruct((d, m, LANES), F32),
        ],
        compiler_params=_params("parallel", "arbitrary"),
        name=f"a_attn_d{d}",
    )(qkv, qkv, qkv, qkv, qkv)


def _a_wo_kernel(o1_ref, o2_ref, o3_ref, m1_ref, m2_ref, m3_ref, l1_ref, l2_ref, l3_ref, h_ref,
                 w_ref, out_ref, nat2_ref, nat3_ref, mn2_ref, mn3_ref, ln2_ref, ln3_ref,
                 xb0_ref, xb1_ref, *, tm, ncol):
    g = pl.program_id(0)
    heads_per_piece = A_HEADS // ncol
    tn = D_MODEL // ncol

    @pl.when(g == 0)
    def _():
        xb1_ref[...] = jnp.zeros(xb1_ref.shape, BF16)

    def body(xb_new, xb_old):
        for o_ref, m_ref, l_ref, nat_ref, mn_ref, ln_ref, (_, d) in (
                (o2_ref, m2_ref, l2_ref, nat2_ref, mn2_ref, ln2_ref, A_GROUPS[1]),
                (o3_ref, m3_ref, l3_ref, nat3_ref, mn3_ref, ln3_ref, A_GROUPS[2])):
            for r in range(d):
                rows = pl.ds(r, tm // d, stride=d)
                mn_ref[0, rows, :] = m_ref[r]
                ln_ref[0, rows, :] = l_ref[r]
                for h in range(A_HEADS):
                    nat_ref[h, rows, :] = o_ref[r, :, h * A_HEAD_DIM:(h + 1) * A_HEAD_DIM].astype(F32)
        m1, m2, m3 = m1_ref[0], mn2_ref[0], mn3_ref[0]
        mx = jnp.maximum(jnp.maximum(m1, m2), m3)
        e1 = jnp.exp2(m1 - mx) * l1_ref[0]
        e2 = jnp.exp2(m2 - mx) * ln2_ref[0]
        e3 = jnp.exp2(m3 - mx) * ln3_ref[0]
        den = e1 + e2 + e3
        w2, w3 = e2 / den, e3 / den
        for piece in range(ncol):
            for h in range(piece * heads_per_piece, (piece + 1) * heads_per_piece):
                sl = slice(h * A_HEAD_DIM, (h + 1) * A_HEAD_DIM)
                o1 = o1_ref[0, :, sl].astype(F32)
                merged = (o1 + w2[:, h:h + 1] * (nat2_ref[h] - o1)
                          + w3[:, h:h + 1] * (nat3_ref[h] - o1))
                xb_new[:, sl] = merged.astype(BF16)
            cols = slice(piece * tn, (piece + 1) * tn)
            out_ref[:, cols] = h_ref[:, cols] + jnp.dot(xb_old[...], w_ref[:, cols],
                                                         preferred_element_type=F32)

    for parity, bufs in enumerate(((xb0_ref, xb1_ref), (xb1_ref, xb0_ref))):
        @pl.when(g % 2 == parity)
        def _(bufs=bufs):
            body(*bufs)


def _a_wo(outs, maxes, dens, h, w_o):
    s = h.shape[0]
    tm = A_WO_TM
    ntiles = s // tm
    cur = lambda g: jnp.minimum(g, ntiles - 1)
    prev = lambda g: jnp.maximum(g - 1, 0)
    o_spec = lambda d: pl.BlockSpec((d, tm // d, A_WIDTH), lambda g: (0, cur(g), 0))
    l_spec = lambda d: pl.BlockSpec((d, tm // d, LANES), lambda g: (0, cur(g), 0))
    ds_ = [d for _, d in A_GROUPS]
    return pl.pallas_call(
        functools.partial(_a_wo_kernel, tm=tm, ncol=A_WO_NCOL),
        grid=(ntiles + 1,),
        in_specs=[o_spec(ds_[0]), o_spec(ds_[1]), o_spec(ds_[2]),
                  l_spec(ds_[0]), l_spec(ds_[1]), l_spec(ds_[2]),
                  l_spec(ds_[0]), l_spec(ds_[1]), l_spec(ds_[2]),
                  pl.BlockSpec((tm, D_MODEL), lambda g: (prev(g), 0)),
                  pl.BlockSpec((A_WIDTH, D_MODEL), lambda g: (0, 0))],
        out_specs=pl.BlockSpec((tm, D_MODEL), lambda g: (prev(g), 0)),
        out_shape=jax.ShapeDtypeStruct((s, D_MODEL), F32),
        scratch_shapes=[
            pltpu.VMEM((A_HEADS, tm, LANES), F32),
            pltpu.VMEM((A_HEADS, tm, LANES), F32),
            pltpu.VMEM((1, tm, LANES), F32),
            pltpu.VMEM((1, tm, LANES), F32),
            pltpu.VMEM((1, tm, LANES), F32),
            pltpu.VMEM((1, tm, LANES), F32),
            pltpu.VMEM((tm, A_WIDTH), BF16),
            pltpu.VMEM((tm, A_WIDTH), BF16),
        ],
        compiler_params=_params("arbitrary"),
        name="a_wo",
    )(*outs, *maxes, *dens, h, w_o)


def _ffn_kernel(h_ref, g_ref, wg_ref, wu_ref, wd_ref, fg_ref, out_ref, xn_ref, *, final_norm):
    f = pl.program_id(1)

    @pl.when(f == 0)
    def _():
        xn_ref[...] = (_rms_scale(h_ref[...]) * g_ref[...]).astype(BF16)
        out_ref[...] = jnp.zeros(out_ref.shape, F32)

    xn = xn_ref[...]
    gate = jnp.dot(xn, wg_ref[...], preferred_element_type=F32)
    up = jnp.dot(xn, wu_ref[...], preferred_element_type=F32)
    act = (gate * jax.nn.sigmoid(gate) * up).astype(BF16)
    out_ref[...] += jnp.dot(act, wd_ref[...], preferred_element_type=F32)

    @pl.when(f == pl.num_programs(1) - 1)
    def _():
        y = h_ref[...] + out_ref[...]
        if final_norm:
            y = _rms_scale(y) * fg_ref[...]
        out_ref[...] = y


def _ffn(h, gain, w_gu, w_down, layer, final_gain, final_norm):
    s = h.shape[0]
    tm, tf = FFN_TM, FFN_TF
    nf = FFN_HIDDEN // tf
    return pl.pallas_call(
        functools.partial(_ffn_kernel, final_norm=final_norm),
        grid=(s // tm, nf),
        in_specs=[
            pl.BlockSpec((tm, D_MODEL), lambda i, f: (i, 0)),
            pl.BlockSpec((1, D_MODEL), lambda i, f: (0, 0)),
            pl.BlockSpec((None, D_MODEL, tf), lambda i, f: (layer, 0, f)),
            pl.BlockSpec((None, D_MODEL, tf), lambda i, f: (layer, 0, nf + f)),
            pl.BlockSpec((None, tf, D_MODEL), lambda i, f: (layer, f, 0)),
            pl.BlockSpec((1, D_MODEL), lambda i, f: (0, 0)),
        ],
        out_specs=pl.BlockSpec((tm, D_MODEL), lambda i, f: (i, 0)),
        out_shape=jax.ShapeDtypeStruct((s, D_MODEL), F32),
        scratch_shapes=[pltpu.VMEM((tm, D_MODEL), BF16)],
        compiler_params=_params("parallel", "arbitrary"),
        name="ffn_final" if final_norm else "ffn",
    )(h, gain, w_gu, w_gu, w_down, final_gain)


def _b_proj_kernel(h_ref, ga_ref, gk_ref, wqa_ref, wkva_ref, gqa_ref, gkva_ref, wqb_ref, wkvb_ref,
                   c_ref, s_ref, ct_ref, st_ref, qt_ref, k_ref, vt_ref):
    y = _rms_scale(h_ref[...])
    xa = (y * ga_ref[...]).astype(BF16)
    xk = (y * gk_ref[...]).astype(BF16)
    c = c_ref[...]
    s_lo, s_hi = _split_sin(s_ref[...], B_ROPE // 2)

    cq = jnp.dot(xa, wqa_ref[...], preferred_element_type=F32)
    cq = (_rms_scale(cq) * gqa_ref[...]).astype(BF16)
    qt = lax.dot_general(wqb_ref[...], cq, (((1,), (1,)), ((), ())),
                         preferred_element_type=F32)
    qscale = (B_NOPE + B_ROPE) ** -0.5 * math.log2(math.e)
    cos_t, sin_t = ct_ref[...], st_ref[...]
    half = B_ROPE // 2
    for h in range(B_HEADS):
        lo = h * B_QK_PAD
        src = h * (B_NOPE + B_ROPE)
        qt_ref[lo:lo + B_NOPE, :] = (qt[src:src + B_NOPE] * qscale).astype(BF16)
        x1 = qt[src + B_NOPE:src + B_NOPE + half]
        x2 = qt[src + B_NOPE + half:src + B_NOPE + B_ROPE]
        qt_ref[lo + B_NOPE:lo + B_NOPE + half, :] = (
            (x1 * cos_t - x2 * sin_t) * qscale).astype(BF16)
        qt_ref[lo + B_NOPE + half:lo + B_NOPE + B_ROPE, :] = (
            (x2 * cos_t + x1 * sin_t) * qscale).astype(BF16)
        qt_ref[lo + B_NOPE + B_ROPE:lo + B_QK_PAD, :] = jnp.zeros(
            (B_QK_PAD - B_NOPE - B_ROPE, qt.shape[1]), BF16)

    ckv = jnp.dot(xk, wkva_ref[...], preferred_element_type=F32)
    lat = (_rms_scale(ckv[:, :B_KV_LORA]) * gkva_ref[...]).astype(BF16)
    k_pe = _rope_tile(ckv[:, B_KV_LORA:], c, s_lo, s_hi, B_ROPE // 2).astype(BF16)
    kv = jnp.dot(lat, wkvb_ref[...], preferred_element_type=F32)
    for h in range(B_HEADS):
        lo = h * B_QK_PAD
        k_ref[:, lo:lo + B_NOPE] = kv[:, lo:lo + B_NOPE].astype(BF16)
        k_ref[:, lo + B_NOPE:lo + B_QK_PAD] = k_pe
        vt_ref[h * B_VE:h * B_VE + B_V, :] = kv[:, lo + B_NOPE:lo + B_QK_PAD].T.astype(BF16)
        vt_ref[h * B_VE + B_V:(h + 1) * B_VE, :] = jnp.ones((B_VE - B_V, h_ref.shape[0]), BF16)


def _b_proj(h, g_attn, g_kv, w_q_a, w_kv_a, g_qa, g_kva, w_q_b_t, w_kv_b, tabs, tabs_t):
    s = h.shape[0]
    tm = B_PROJ_TM
    full = lambda a: pl.BlockSpec(a.shape, lambda i: (0,) * a.ndim)
    row = lambda width: pl.BlockSpec((tm, width), lambda i: (i, 0))
    col = lambda height: pl.BlockSpec((height, tm), lambda i: (0, i))
    return pl.pallas_call(
        _b_proj_kernel,
        grid=(s // tm,),
        in_specs=[row(D_MODEL), full(g_attn), full(g_kv), full(w_q_a), full(w_kv_a), full(g_qa),
                  full(g_kva), full(w_q_b_t), full(w_kv_b), row(LANES), row(LANES),
                  col(B_ROPE // 2), col(B_ROPE // 2)],
        out_specs=[col(B_HEADS * B_QK_PAD), row(B_HEADS * B_QK_PAD), col(B_HEADS * B_VE)],
        out_shape=[
            jax.ShapeDtypeStruct((B_HEADS * B_QK_PAD, s), BF16),
            jax.ShapeDtypeStruct((s, B_HEADS * B_QK_PAD), BF16),
            jax.ShapeDtypeStruct((B_HEADS * B_VE, s), BF16),
        ],
        compiler_params=_params("parallel"),
        name="b_proj",
    )(h, g_attn, g_kv, w_q_a, w_kv_a, g_qa, g_kva, w_q_b_t, w_kv_b, *tabs, *tabs_t)


def _b_attn_kernel(qi_ref, ki_ref, hh_ref, qt_ref, k_ref, vt_ref, o_ref, s_ref, m_ref,
                   alpha_ref, acc_ref, *, t, cq, kb, kp, hp, nitems):
    g = pl.program_id(0)
    nchunk = t // cq
    a = jnp.minimum(g, nitems - 1)
    b = jnp.maximum(g - 1, 0)
    qi_a, ki_a = qi_ref[a], ki_ref[a]

    @pl.when(g == 0)
    def _():
        s_ref[...] = jnp.zeros(s_ref.shape, F32)
        m_ref[...] = jnp.zeros(m_ref.shape, F32)
        alpha_ref[...] = jnp.zeros(alpha_ref.shape, F32)
        acc_ref[...] = jnp.zeros(acc_ref.shape, F32)

    def body(diag_a, diag_b):
        for hd, c in [(hd, c) for hd in range(hp) for c in range(nchunk)]:
            qk = slice(hd * B_QK_PAD, (hd + 1) * B_QK_PAD)
            ve = slice(hd * B_VE, (hd + 1) * B_VE)
            m_b = m_ref[hd, c]
            rows_a = (c + 1) * cq if diag_a else t
            rows_b = (c + 1) * cq if diag_b else t
            cmax, pv = None, None
            for i in range(t // kp):
                for j in range(i * kp // kb, min((i + 1) * kp, rows_b) // kb):
                    pt = jnp.exp2(s_ref[hd, c, j * kb:(j + 1) * kb, :] - m_b)
                    d = jnp.dot(vt_ref[ve, j * kb:(j + 1) * kb], pt.astype(BF16),
                                preferred_element_type=F32)
                    pv = d if pv is None else pv + d
                if i * kp < rows_a:
                    s = jnp.dot(k_ref[i * kp:(i + 1) * kp, qk], qt_ref[qk, c * cq:(c + 1) * cq],
                                preferred_element_type=F32)
                    if diag_a and (i + 1) * kp > c * cq:
                        k_pos = i * kp + lax.broadcasted_iota(jnp.int32, (kp, cq), 0)
                        q_pos = c * cq + lax.broadcasted_iota(jnp.int32, (kp, cq), 1)
                        s = jnp.where(k_pos <= q_pos, s, -jnp.inf)
                    s_ref[hd, c, i * kp:(i + 1) * kp, :] = s
                    pmax = jnp.max(s, axis=0, keepdims=True)
                    cmax = pmax if cmax is None else jnp.maximum(cmax, pmax)
            acc_ref[hd, c] = alpha_ref[hd, c] * acc_ref[hd, c] + pv
            m_prev = jnp.where(ki_a == 0, -jnp.inf, m_b)
            m_new = jnp.maximum(m_prev, cmax)
            m_ref[hd, c] = m_new
            alpha_ref[hd, c] = jnp.exp2(m_prev - m_new)

    is_diag_b = ki_ref[b] == qi_ref[b]
    for diag_a in (False, True):
        for diag_b in (False, True):
            @pl.when(((ki_a == qi_a) == diag_a) & (is_diag_b == diag_b))
            def _(diag_a=diag_a, diag_b=diag_b):
                body(diag_a, diag_b)

    @pl.when(is_diag_b)
    def _():
        for hd in range(hp):
            for c in range(nchunk):
                acc = acc_ref[hd, c]
                o = acc[:B_V] / acc[B_V:B_V + 1]
                o_ref[c * cq:(c + 1) * cq, hd * B_V:(hd + 1) * B_V] = o.T.astype(o_ref.dtype)


def _b_attn(qt, k, vt):
    s = k.shape[0]
    t, cq, hp = B_ATT_T, B_ATT_CQ, B_ATT_HP
    items = [(h, a, b) for h in range(B_HEADS // hp) for a in range(s // t) for b in range(a + 1)]
    n = len(items)
    hh, qi, ki = (jnp.asarray(np.array(col, np.int32)) for col in zip(*items))
    cur = lambda g: jnp.minimum(g, n - 1)
    prev = lambda g: jnp.maximum(g - 1, 0)
    nchunk = t // cq
    grid_spec = pltpu.PrefetchScalarGridSpec(
        num_scalar_prefetch=3,
        grid=(n + 1,),
        in_specs=[
            pl.BlockSpec((hp * B_QK_PAD, t), lambda g, qi, ki, hh: (hh[cur(g)], qi[cur(g)])),
            pl.BlockSpec((t, hp * B_QK_PAD), lambda g, qi, ki, hh: (ki[cur(g)], hh[cur(g)])),
            pl.BlockSpec((hp * B_VE, t), lambda g, qi, ki, hh: (hh[prev(g)], ki[prev(g)])),
        ],
        out_specs=pl.BlockSpec((t, hp * B_V), lambda g, qi, ki, hh: (qi[prev(g)], hh[prev(g)])),
        scratch_shapes=[
            pltpu.VMEM((hp, nchunk, t, cq), F32),
            pltpu.VMEM((hp, nchunk, 1, cq), F32),
            pltpu.VMEM((hp, nchunk, 1, cq), F32),
            pltpu.VMEM((hp, nchunk, B_VE, cq), F32),
        ],
    )
    return pl.pallas_call(
        functools.partial(_b_attn_kernel, t=t, cq=cq, kb=B_ATT_KB, kp=B_ATT_KP, hp=hp, nitems=n),
        grid_spec=grid_spec,
        out_shape=jax.ShapeDtypeStruct((s, B_HEADS * B_V), BF16),
        compiler_params=_params("arbitrary"),
        name="b_attn",
    )(qi, ki, hh, qt, k, vt)


def _b_wo_kernel(o_ref, h_ref, w_ref, out_ref):
    out_ref[...] = h_ref[...] + jnp.dot(o_ref[...], w_ref[...], preferred_element_type=F32)


def _b_wo(o, h, w_o):
    s = h.shape[0]
    tm = B_WO_TM
    return pl.pallas_call(
        _b_wo_kernel,
        grid=(s // tm,),
        in_specs=[pl.BlockSpec((tm, B_HEADS * B_V), lambda i: (i, 0)),
                  pl.BlockSpec((tm, D_MODEL), lambda i: (i, 0)),
                  pl.BlockSpec((B_HEADS * B_V, D_MODEL), lambda i: (0, 0))],
        out_specs=pl.BlockSpec((tm, D_MODEL), lambda i: (i, 0)),
        out_shape=jax.ShapeDtypeStruct((s, D_MODEL), F32),
        compiler_params=_params("parallel"),
        name="b_wo",
    )(o, h, w_o)


def _rope_freq(dim):
    return ROPE_THETA ** (-jnp.arange(0, dim, 2, dtype=F32) / dim)


def _lane_freq(dim):
    f = _rope_freq(dim)
    return jnp.concatenate([f, f, jnp.zeros((LANES - dim,), F32)])


def _angle_parts(seq_len, tile, offsets, freq):
    start = (jnp.arange(seq_len // tile, dtype=F32) * tile)[:, None] * freq[None, :]
    off = jnp.asarray(offsets, F32)[:, None] * freq[None, :]
    return jnp.cos(start), jnp.sin(start), jnp.cos(off), jnp.sin(off)


def _rope_tables(seq_len, tile, offsets, freq):
    ca, sa, cb, sb = _angle_parts(seq_len, tile, offsets, freq)
    cos = ca[:, None, :] * cb[None] - sa[:, None, :] * sb[None]
    sin = sa[:, None, :] * cb[None] + ca[:, None, :] * sb[None]
    return cos.reshape(seq_len, -1), sin.reshape(seq_len, -1)


def _rope_tables_t(seq_len, tile, freq):
    ca, sa, cb, sb = (t.T for t in _angle_parts(seq_len, tile, np.arange(tile), freq))
    cos = ca[:, :, None] * cb[:, None, :] - sa[:, :, None] * sb[:, None, :]
    sin = sa[:, :, None] * cb[:, None, :] + ca[:, :, None] * sb[:, None, :]
    return cos.reshape(-1, seq_len), sin.reshape(-1, seq_len)


def kernel(x, attn_norm_g, ffn_norm_g, a_w_qkv, a_w_o, kv_norm_g, b_w_kv_a, b_kv_a_norm_g, b_w_kv_b,
           b_w_q_a, b_q_a_norm_g, b_w_q_b, b_w_o, ffn_w_gu, ffn_w_down, final_norm_g):
    assert x.shape[0] == 1 and x.shape[2] == D_MODEL
    s = x.shape[1]
    assert s % QKV_TM == 0 and s % (A_GROUPS[-1][1] * A_ATT_TQ) == 0 and s % B_ATT_T == 0
    h = x[0]
    row = lambda g: g.reshape(1, -1).astype(F32)
    tabs_b = _rope_tables(s, B_PROJ_TM, np.arange(B_PROJ_TM), _lane_freq(B_ROPE))
    tabs_bt = _rope_tables_t(s, B_PROJ_TM, _rope_freq(B_ROPE))

    w_qkv = a_w_qkv[0].astype(BF16)
    outs, maxes, dens = [], [], []
    for group, (_, d) in enumerate(A_GROUPS):
        qkv = _a_qkv(h, row(attn_norm_g[0]), w_qkv, group, d)
        o, row_max, den = _a_attn(qkv, d)
        outs.append(o)
        maxes.append(row_max)
        dens.append(den)
    h = _a_wo(outs, maxes, dens, h, a_w_o[0].astype(BF16))
    w_gu, w_down = ffn_w_gu.astype(BF16), ffn_w_down.astype(BF16)
    h = _ffn(h, row(ffn_norm_g[0]), w_gu, w_down, 0, row(final_norm_g), final_norm=False)

    w_kv_a = jnp.pad(b_w_kv_a, ((0, 0), (0, LANES - B_ROPE))).astype(BF16)
    qt, k, vt = _b_proj(h, row(attn_norm_g[1]), row(kv_norm_g), b_w_q_a[0].astype(BF16), w_kv_a,
                        row(b_q_a_norm_g[0]), row(b_kv_a_norm_g), b_w_q_b[0].T.astype(BF16),
                        b_w_kv_b.astype(BF16), tabs_b, tabs_bt)
    o = _b_attn(qt, k, vt)
    h = _b_wo(o, h, b_w_o[0].astype(BF16))
    h = _ffn(h, row(ffn_norm_g[1]), w_gu, w_down, 1, row(final_norm_g), final_norm=True)
    return h[None]
```

```python
import functools
import math

import jax
import jax.numpy as jnp
import numpy as np
from jax import lax
from jax.experimental import pallas as pl
from jax.experimental.pallas import tpu as pltpu

F32 = jnp.float32
BF16 = jnp.bfloat16

D_MODEL = 2048
ROPE_THETA = 500000.0
NORM_EPS = 1e-6
LANES = 128

A_HEAD_DIM = 128
A_HEADS = D_MODEL // A_HEAD_DIM
A_ROT_DIM = A_HEAD_DIM // 4
A_GROUPS = ((128, 1), (512, 4), (2048, 16))
A_WIDTH = A_HEADS * A_HEAD_DIM
A_SPAN_BLK = 128

B_HEADS = D_MODEL // 128
B_NOPE = 128
B_ROPE = 64
B_V = 128
B_VE = B_V + 16
B_KV_LORA = 512
B_QK_PAD = 256

FFN_HIDDEN = 5632

VMEM_LIMIT = 56 * 1024 * 1024

QKV_TM, QKV_TN = 1024, 1024
QKV_XSLABS = 4
A_ATT_TQ = 512
A_WO_TM = 256
A_WO_NCOL = 4
FFN_TM, FFN_TF = 1024, 512
B_PROJ_TM = 256
B_ATT_T, B_ATT_CQ = 2048, 512
B_ATT_HP = 2
B_ATT_KB = 256
B_ATT_KP = 512
B_WO_TM = 512


def _params(*sem):
    return pltpu.CompilerParams(dimension_semantics=sem, vmem_limit_bytes=VMEM_LIMIT)


def _rms_scale(x):
    return x * lax.rsqrt(jnp.mean(x * x, axis=-1, keepdims=True) + NORM_EPS)


def _split_sin(s, half):
    lane = lax.broadcasted_iota(jnp.int32, s.shape, 1)
    return jnp.where(lane < half, -s, 0.0), jnp.where(lane >= half, s, 0.0)


def _rope_tile(a, c, s_lo, s_hi, half):
    return (a * c + pltpu.roll(a, LANES - half, 1) * s_lo + pltpu.roll(a, half, 1) * s_hi)


def _a_qkv_kernel(x_ref, g_ref, w_ref, c_ref, s_ref, o_ref, xn_ref, acc_ref, *xs,
                  d, tm, tn, ncol, nsteps):
    g = pl.program_id(0)
    sub = tm // d

    @pl.when(g == 0)
    def _():
        acc_ref[...] = jnp.zeros(acc_ref.shape, F32)

    @pl.when((g % ncol == 0) & (g < nsteps - 1))
    def _():
        if d == 1:
            xn_ref[...] = (_rms_scale(x_ref[...]) * g_ref[...]).astype(BF16)
        else:
            x = x_ref[...]
            inv = lax.rsqrt(jnp.mean(x * x, axis=-1, keepdims=True) + NORM_EPS)
            nslab = xs[0].shape[0]
            for base in range(0, D_MODEL // LANES, nslab):
                for k in range(nslab):
                    cols = slice((base + k) * LANES, (base + k + 1) * LANES)
                    xs[0][k] = x_ref[:, cols] * inv * g_ref[:, cols]
                for k in range(nslab):
                    for r in range(d):
                        xn_ref[r * sub:(r + 1) * sub, (base + k) * LANES:(base + k + 1) * LANES] = (
                            xs[0][k, pl.ds(r, sub, stride=d), :].astype(BF16))

    t = (jnp.maximum(g - 1, 0) % ncol) // (A_WIDTH // tn)
    rot = t < 2
    qscale = jnp.where(t == 0, A_HEAD_DIM ** -0.5 * math.log2(math.e), 1.0).astype(F32)
    c = jnp.where(rot, c_ref[...], 1.0) * qscale
    s_lo, s_hi = _split_sin(jnp.where(rot, s_ref[...], 0.0) * qscale, A_ROT_DIM // 2)
    for col in range(tn // LANES):
        sl = slice(col * LANES, (col + 1) * LANES)
        a = _rope_tile(acc_ref[:, sl], c, s_lo, s_hi, A_ROT_DIM // 2).astype(BF16)
        for r in range(d):
            o_ref[r, :, sl] = a[r * sub:(r + 1) * sub]

    acc_ref[...] = jnp.dot(xn_ref[...], w_ref[...], preferred_element_type=F32)


def _a_qkv(x, gain, w_qkv, group, d):
    s = x.shape[0]
    tm, tn = QKV_TM, QKV_TN
    ncol = 3 * A_WIDTH // tn
    nsteps = (s // tm) * ncol + 1
    cur = lambda g: jnp.minimum(g, nsteps - 2)
    prev = lambda g: jnp.maximum(g - 1, 0)
    tab_spec = pl.BlockSpec((tm, LANES), lambda g: (prev(g) // ncol, 0))
    offsets = np.arange(tm).reshape(tm // d, d).T.reshape(-1)
    tabs = _rope_tables(s, tm, offsets, _lane_freq(A_ROT_DIM))
    scratch = [pltpu.VMEM((tm, D_MODEL), BF16), pltpu.VMEM((tm, tn), F32)]
    if d > 1:
        scratch.append(pltpu.VMEM((QKV_XSLABS, tm, LANES), F32))
    return pl.pallas_call(
        functools.partial(_a_qkv_kernel, d=d, tm=tm, tn=tn, ncol=ncol, nsteps=nsteps),
        grid=(nsteps,),
        in_specs=[
            pl.BlockSpec((tm, D_MODEL), lambda g: (cur(g) // ncol, 0)),
            pl.BlockSpec((1, D_MODEL), lambda g: (0, 0)),
            pl.BlockSpec((D_MODEL, tn), lambda g: (0, group * ncol + cur(g) % ncol)),
            tab_spec, tab_spec,
        ],
        out_specs=pl.BlockSpec((d, tm // d, tn), lambda g: (0, prev(g) // ncol, prev(g) % ncol)),
        out_shape=jax.ShapeDtypeStruct((d, s // d, 3 * A_WIDTH), BF16),
        scratch_shapes=scratch,
        compiler_params=_params("arbitrary"),
        name=f"a_qkv_d{d}",
    )(x, gain, w_qkv, *tabs)


def _a_attn_kernel(q_ref, kp_ref, kc_ref, vp_ref, vc_ref, o_ref, m_ref, l_ref, *, tq):
    n = pl.program_id(1)
    blk = A_SPAN_BLK
    qi = lax.broadcasted_iota(jnp.int32, (blk, 2 * blk), 0)
    kj = lax.broadcasted_iota(jnp.int32, (blk, 2 * blk), 1)
    band = (kj >= qi) & (kj <= qi + blk)
    band_first = band & ((kj >= blk) | (n > 0))
    bias = jnp.where(band, 0.0, -jnp.inf)
    bias_first = jnp.where(band_first, 0.0, -jnp.inf)
    lane = lax.broadcasted_iota(jnp.int32, (blk, LANES), 1)
    ones = jnp.ones((2 * blk, LANES), BF16)
    nt = (((1,), (1,)), ((), ()))
    for b in range(tq // blk):
        rows = slice(b * blk, (b + 1) * blk)
        m_tile = jnp.zeros((blk, LANES), F32)
        l_tile = jnp.ones((blk, LANES), F32)
        for h in range(A_HEADS):
            sl = slice(h * A_HEAD_DIM, (h + 1) * A_HEAD_DIM)
            if b == 0:
                k = jnp.concatenate([kp_ref[:, sl], kc_ref[0:blk, sl]], axis=0)
                v = jnp.concatenate([vp_ref[:, sl], vc_ref[0:blk, sl]], axis=0)
            else:
                k = kc_ref[(b - 1) * blk:(b + 1) * blk, sl]
                v = vc_ref[(b - 1) * blk:(b + 1) * blk, sl]
            s = lax.dot_general(q_ref[rows, sl], k, nt, preferred_element_type=F32)
            s = s + (bias_first if b == 0 else bias)
            m = jnp.max(jnp.maximum(s[:, :blk], s[:, blk:]), axis=-1, keepdims=True)
            p = jnp.exp2(s - m).astype(BF16)
            oe = jnp.dot(p, jnp.concatenate([v, ones], axis=1), preferred_element_type=F32)
            l = oe[:, A_HEAD_DIM:]
            o_ref[rows, sl] = (oe[:, :A_HEAD_DIM] / l).astype(o_ref.dtype)
            m_tile = jnp.where(lane == h, m, m_tile)
            l_tile = jnp.where(lane == h, l, l_tile)
        m_ref[rows, :] = m_tile
        l_ref[rows, :] = l_tile


def _a_attn(qkv, d):
    _, m, _ = qkv.shape
    tq = A_ATT_TQ
    ratio = tq // A_SPAN_BLK
    cur = lambda col: pl.BlockSpec((None, tq, A_WIDTH), lambda r, n: (r, n, col))
    prev = lambda col: pl.BlockSpec(
        (None, A_SPAN_BLK, A_WIDTH), lambda r, n: (r, jnp.maximum(n * ratio - 1, 0), col))
    return pl.pallas_call(
        functools.partial(_a_attn_kernel, tq=tq),
        grid=(d, m // tq),
        in_specs=[cur(0), prev(1), cur(1), prev(2), cur(2)],
        out_specs=[
            pl.BlockSpec((None, tq, A_WIDTH), lambda r, n: (r, n, 0)),
            pl.BlockSpec((None, tq, LANES), lambda r, n: (r, n, 0)),
            pl.BlockSpec((None, tq, LANES), lambda r, n: (r, n, 0)),
        ],
        out_shape=[
            jax.ShapeDtypeStruct((d, m, A_WIDTH), BF16),
            jax.ShapeDtypeStruct((d, m, LANES), F32),
            jax.ShapeDtypeStruct((d, m, LANES), F32),
        ],
        compiler_params=_params("parallel", "arbitrary"),
        name=f"a_attn_d{d}",
    )(qkv, qkv, qkv, qkv, qkv)


def _a_wo_kernel(o1_ref, o2_ref, o3_ref, m1_ref, m2_ref, m3_ref, l1_ref, l2_ref, l3_ref, h_ref,
                 w_ref, out_ref, nat2_ref, nat3_ref, mn2_ref, mn3_ref, ln2_ref, ln3_ref,
                 xb0_ref, xb1_ref, *, tm, ncol):
    g = pl.program_id(0)
    heads_per_piece = A_HEADS // ncol
    tn = D_MODEL // ncol

    @pl.when(g == 0)
    def _():
        xb1_ref[...] = jnp.zeros(xb1_ref.shape, BF16)

    def body(xb_new, xb_old):
        for o_ref, m_ref, l_ref, nat_ref, mn_ref, ln_ref, (_, d) in (
                (o2_ref, m2_ref, l2_ref, nat2_ref, mn2_ref, ln2_ref, A_GROUPS[1]),
                (o3_ref, m3_ref, l3_ref, nat3_ref, mn3_ref, ln3_ref, A_GROUPS[2])):
            for r in range(d):
                rows = pl.ds(r, tm // d, stride=d)
                mn_ref[0, rows, :] = m_ref[r]
                ln_ref[0, rows, :] = l_ref[r]
                for h in range(A_HEADS):
                    nat_ref[h, rows, :] = o_ref[r, :, h * A_HEAD_DIM:(h + 1) * A_HEAD_DIM].astype(F32)
        m1, m2, m3 = m1_ref[0], mn2_ref[0], mn3_ref[0]
        mx = jnp.maximum(jnp.maximum(m1, m2), m3)
        e1 = jnp.exp2(m1 - mx) * l1_ref[0]
        e2 = jnp.exp2(m2 - mx) * ln2_ref[0]
        e3 = jnp.exp2(m3 - mx) * ln3_ref[0]
        den = e1 + e2 + e3
        w2, w3 = e2 / den, e3 / den
        for piece in range(ncol):
            for h in range(piece * heads_per_piece, (piece + 1) * heads_per_piece):
                sl = slice(h * A_HEAD_DIM, (h + 1) * A_HEAD_DIM)
                o1 = o1_ref[0, :, sl].astype(F32)
                merged = (o1 + w2[:, h:h + 1] * (nat2_ref[h] - o1)
                          + w3[:, h:h + 1] * (nat3_ref[h] - o1))
                xb_new[:, sl] = merged.astype(BF16)
            cols = slice(piece * tn, (piece + 1) * tn)
            out_ref[:, cols] = h_ref[:, cols] + jnp.dot(xb_old[...], w_ref[:, cols],
                                                         preferred_element_type=F32)

    for parity, bufs in enumerate(((xb0_ref, xb1_ref), (xb1_ref, xb0_ref))):
        @pl.when(g % 2 == parity)
        def _(bufs=bufs):
            body(*bufs)


def _a_wo(outs, maxes, dens, h, w_o):
    s = h.shape[0]
    tm = A_WO_TM
    ntiles = s // tm
    cur = lambda g: jnp.minimum(g, ntiles - 1)
    prev = lambda g: jnp.maximum(g - 1, 0)
    o_spec = lambda d: pl.BlockSpec((d, tm // d, A_WIDTH), lambda g: (0, cur(g), 0))
    l_spec = lambda d: pl.BlockSpec((d, tm // d, LANES), lambda g: (0, cur(g), 0))
    ds_ = [d for _, d in A_GROUPS]
    return pl.pallas_call(
        functools.partial(_a_wo_kernel, tm=tm, ncol=A_WO_NCOL),
        grid=(ntiles + 1,),
        in_specs=[o_spec(ds_[0]), o_spec(ds_[1]), o_spec(ds_[2]),
                  l_spec(ds_[0]), l_spec(ds_[1]), l_spec(ds_[2]),
                  l_spec(ds_[0]), l_spec(ds_[1]), l_spec(ds_[2]),
                  pl.BlockSpec((tm, D_MODEL), lambda g: (prev(g), 0)),
                  pl.BlockSpec((A_WIDTH, D_MODEL), lambda g: (0, 0))],
        out_specs=pl.BlockSpec((tm, D_MODEL), lambda g: (prev(g), 0)),
        out_shape=jax.ShapeDtypeStruct((s, D_MODEL), F32),
        scratch_shapes=[
            pltpu.VMEM((A_HEADS, tm, LANES), F32),
            pltpu.VMEM((A_HEADS, tm, LANES), F32),
            pltpu.VMEM((1, tm, LANES), F32),
            pltpu.VMEM((1, tm, LANES), F32),
            pltpu.VMEM((1, tm, LANES), F32),
            pltpu.VMEM((1, tm, LANES), F32),
            pltpu.VMEM((tm, A_WIDTH), BF16),
            pltpu.VMEM((tm, A_WIDTH), BF16),
        ],
        compiler_params=_params("arbitrary"),
        name="a_wo",
    )(*outs, *maxes, *dens, h, w_o)


def _ffn_kernel(h_hbm, g_ref, wg_ref, wu_ref, wd_ref, fg_ref, out_ref, xn_ref, hbuf_ref, sem,
                *, tm, final_norm):
    i, f = pl.program_id(0), pl.program_id(1)
    fetch = lambda tile: pltpu.make_async_copy(h_hbm.at[pl.ds(tile * tm, tm), :], hbuf_ref, sem)

    @pl.when((i == 0) & (f == 0))
    def _():
        fetch(0).start()

    @pl.when(f == 0)
    def _():
        fetch(i).wait()
        h = hbuf_ref[...]
        xn_ref[...] = (_rms_scale(h) * g_ref[...]).astype(BF16)
        out_ref[...] = h

    @pl.when((f == 1) & (i + 1 < pl.num_programs(0)))
    def _():
        fetch(i + 1).start()

    xn = xn_ref[...]
    gate = jnp.dot(xn, wg_ref[...], preferred_element_type=F32)
    up = jnp.dot(xn, wu_ref[...], preferred_element_type=F32)
    act = (gate * jax.nn.sigmoid(gate) * up).astype(BF16)
    out_ref[...] += jnp.dot(act, wd_ref[...], preferred_element_type=F32)

    if final_norm:
        @pl.when(f == pl.num_programs(1) - 1)
        def _():
            out_ref[...] = _rms_scale(out_ref[...]) * fg_ref[...]


def _ffn(h, gain, w_gu, w_down, layer, final_gain, final_norm):
    s = h.shape[0]
    tm, tf = FFN_TM, FFN_TF
    nf = FFN_HIDDEN // tf
    return pl.pallas_call(
        functools.partial(_ffn_kernel, tm=tm, final_norm=final_norm),
        grid=(s // tm, nf),
        in_specs=[
            pl.BlockSpec(memory_space=pl.ANY),
            pl.BlockSpec((1, D_MODEL), lambda i, f: (0, 0)),
            pl.BlockSpec((None, D_MODEL, tf), lambda i, f: (layer, 0, f)),
            pl.BlockSpec((None, D_MODEL, tf), lambda i, f: (layer, 0, nf + f)),
            pl.BlockSpec((None, tf, D_MODEL), lambda i, f: (layer, f, 0)),
            pl.BlockSpec((1, D_MODEL), lambda i, f: (0, 0)),
        ],
        out_specs=pl.BlockSpec((tm, D_MODEL), lambda i, f: (i, 0)),
        out_shape=jax.ShapeDtypeStruct((s, D_MODEL), F32),
        scratch_shapes=[pltpu.VMEM((tm, D_MODEL), BF16), pltpu.VMEM((tm, D_MODEL), F32),
                        pltpu.SemaphoreType.DMA(())],
        compiler_params=_params("arbitrary", "arbitrary"),
        name="ffn_final" if final_norm else "ffn",
    )(h, gain, w_gu, w_gu, w_down, final_gain)


def _b_proj_kernel(h_ref, ga_ref, gk_ref, wqa_ref, wkva_ref, gqa_ref, gkva_ref, wqb_ref, wkvb_ref,
                   c_ref, s_ref, ct_ref, st_ref, qt_ref, k_ref, vt_ref):
    y = _rms_scale(h_ref[...])
    xa = (y * ga_ref[...]).astype(BF16)
    xk = (y * gk_ref[...]).astype(BF16)
    c = c_ref[...]
    s_lo, s_hi = _split_sin(s_ref[...], B_ROPE // 2)

    cq = jnp.dot(xa, wqa_ref[...], preferred_element_type=F32)
    cq = (_rms_scale(cq) * gqa_ref[...]).astype(BF16)
    qt = lax.dot_general(wqb_ref[...], cq, (((1,), (1,)), ((), ())),
                         preferred_element_type=F32)
    qscale = (B_NOPE + B_ROPE) ** -0.5 * math.log2(math.e)
    cos_t, sin_t = ct_ref[...], st_ref[...]
    half = B_ROPE // 2
    for h in range(B_HEADS):
        lo = h * B_QK_PAD
        src = h * (B_NOPE + B_ROPE)
        qt_ref[lo:lo + B_NOPE, :] = (qt[src:src + B_NOPE] * qscale).astype(BF16)
        x1 = qt[src + B_NOPE:src + B_NOPE + half]
        x2 = qt[src + B_NOPE + half:src + B_NOPE + B_ROPE]
        qt_ref[lo + B_NOPE:lo + B_NOPE + half, :] = (
            (x1 * cos_t - x2 * sin_t) * qscale).astype(BF16)
        qt_ref[lo + B_NOPE + half:lo + B_NOPE + B_ROPE, :] = (
            (x2 * cos_t + x1 * sin_t) * qscale).astype(BF16)
        qt_ref[lo + B_NOPE + B_ROPE:lo + B_QK_PAD, :] = jnp.zeros(
            (B_QK_PAD - B_NOPE - B_ROPE, qt.shape[1]), BF16)

    ckv = jnp.dot(xk, wkva_ref[...], preferred_element_type=F32)
    lat = (_rms_scale(ckv[:, :B_KV_LORA]) * gkva_ref[...]).astype(BF16)
    k_pe = _rope_tile(ckv[:, B_KV_LORA:], c, s_lo, s_hi, B_ROPE // 2).astype(BF16)
    kv = jnp.dot(lat, wkvb_ref[...], preferred_element_type=F32)
    for h in range(B_HEADS):
        lo = h * B_QK_PAD
        k_ref[:, lo:lo + B_NOPE] = kv[:, lo:lo + B_NOPE].astype(BF16)
        k_ref[:, lo + B_NOPE:lo + B_QK_PAD] = k_pe
        vt_ref[h * B_VE:h * B_VE + B_V, :] = kv[:, lo + B_NOPE:lo + B_QK_PAD].T.astype(BF16)
        vt_ref[h * B_VE + B_V:(h + 1) * B_VE, :] = jnp.ones((B_VE - B_V, h_ref.shape[0]), BF16)


def _b_proj(h, g_attn, g_kv, w_q_a, w_kv_a, g_qa, g_kva, w_q_b_t, w_kv_b, tabs, tabs_t):
    s = h.shape[0]
    tm = B_PROJ_TM
    full = lambda a: pl.BlockSpec(a.shape, lambda i: (0,) * a.ndim)
    row = lambda width: pl.BlockSpec((tm, width), lambda i: (i, 0))
    col = lambda height: pl.BlockSpec((height, tm), lambda i: (0, i))
    return pl.pallas_call(
        _b_proj_kernel,
        grid=(s // tm,),
        in_specs=[row(D_MODEL), full(g_attn), full(g_kv), full(w_q_a), full(w_kv_a), full(g_qa),
                  full(g_kva), full(w_q_b_t), full(w_kv_b), row(LANES), row(LANES),
                  col(B_ROPE // 2), col(B_ROPE // 2)],
        out_specs=[col(B_HEADS * B_QK_PAD), row(B_HEADS * B_QK_PAD), col(B_HEADS * B_VE)],
        out_shape=[
            jax.ShapeDtypeStruct((B_HEADS * B_QK_PAD, s), BF16),
            jax.ShapeDtypeStruct((s, B_HEADS * B_QK_PAD), BF16),
            jax.ShapeDtypeStruct((B_HEADS * B_VE, s), BF16),
        ],
        compiler_params=_params("parallel"),
        name="b_proj",
    )(h, g_attn, g_kv, w_q_a, w_kv_a, g_qa, g_kva, w_q_b_t, w_kv_b, *tabs, *tabs_t)


def _b_attn_kernel(qi_ref, ki_ref, hh_ref, qt_ref, k_ref, vt_ref, o_ref, s_ref, m_ref,
                   alpha_ref, acc_ref, *, t, cq, kb, kp, hp, nitems):
    g = pl.program_id(0)
    nchunk = t // cq
    a = jnp.minimum(g, nitems - 1)
    b = jnp.maximum(g - 1, 0)
    qi_a, ki_a = qi_ref[a], ki_ref[a]

    @pl.when(g == 0)
    def _():
        s_ref[...] = jnp.zeros(s_ref.shape, F32)
        m_ref[...] = jnp.zeros(m_ref.shape, F32)
        alpha_ref[...] = jnp.zeros(alpha_ref.shape, F32)
        acc_ref[...] = jnp.zeros(acc_ref.shape, F32)

    def body(diag_a, diag_b):
        for hd, c in [(hd, c) for hd in range(hp) for c in range(nchunk)]:
            qk = slice(hd * B_QK_PAD, (hd + 1) * B_QK_PAD)
            ve = slice(hd * B_VE, (hd + 1) * B_VE)
            m_b = m_ref[hd, c]
            rows_a = (c + 1) * cq if diag_a else t
            rows_b = (c + 1) * cq if diag_b else t
            cmax, pv = None, None
            for i in range(t // kp):
                for j in range(i * kp // kb, min((i + 1) * kp, rows_b) // kb):
                    pt = jnp.exp2(s_ref[hd, c, j * kb:(j + 1) * kb, :] - m_b)
                    d = jnp.dot(vt_ref[ve, j * kb:(j + 1) * kb], pt.astype(BF16),
                                preferred_element_type=F32)
                    pv = d if pv is None else pv + d
                if i * kp < rows_a:
                    s = jnp.dot(k_ref[i * kp:(i + 1) * kp, qk], qt_ref[qk, c * cq:(c + 1) * cq],
                                preferred_element_type=F32)
                    if diag_a and (i + 1) * kp > c * cq:
                        k_pos = i * kp + lax.broadcasted_iota(jnp.int32, (kp, cq), 0)
                        q_pos = c * cq + lax.broadcasted_iota(jnp.int32, (kp, cq), 1)
                        s = jnp.where(k_pos <= q_pos, s, -jnp.inf)
                    s_ref[hd, c, i * kp:(i + 1) * kp, :] = s
                    pmax = jnp.max(s, axis=0, keepdims=True)
                    cmax = pmax if cmax is None else jnp.maximum(cmax, pmax)
            acc_ref[hd, c] = alpha_ref[hd, c] * acc_ref[hd, c] + pv
            m_prev = jnp.where(ki_a == 0, -jnp.inf, m_b)
            m_new = jnp.maximum(m_prev, cmax)
            m_ref[hd, c] = m_new
            alpha_ref[hd, c] = jnp.exp2(m_prev - m_new)

    is_diag_b = ki_ref[b] == qi_ref[b]
    for diag_a in (False, True):
        for diag_b in (False, True):
            @pl.when(((ki_a == qi_a) == diag_a) & (is_diag_b == diag_b))
            def _(diag_a=diag_a, diag_b=diag_b):
                body(diag_a, diag_b)

    @pl.when(is_diag_b)
    def _():
        for hd in range(hp):
            for c in range(nchunk):
                acc = acc_ref[hd, c]
                o = acc[:B_V] / acc[B_V:B_V + 1]
                o_ref[c * cq:(c + 1) * cq, hd * B_V:(hd + 1) * B_V] = o.T.astype(o_ref.dtype)


def _b_attn(qt, k, vt):
    s = k.shape[0]
    t, cq, hp = B_ATT_T, B_ATT_CQ, B_ATT_HP
    items = [(h, a, b) for h in range(B_HEADS // hp) for a in range(s // t) for b in range(a + 1)]
    n = len(items)
    hh, qi, ki = (jnp.asarray(np.array(col, np.int32)) for col in zip(*items))
    cur = lambda g: jnp.minimum(g, n - 1)
    prev = lambda g: jnp.maximum(g - 1, 0)
    nchunk = t // cq
    grid_spec = pltpu.PrefetchScalarGridSpec(
        num_scalar_prefetch=3,
        grid=(n + 1,),
        in_specs=[
            pl.BlockSpec((hp * B_QK_PAD, t), lambda g, qi, ki, hh: (hh[cur(g)], qi[cur(g)])),
            pl.BlockSpec((t, hp * B_QK_PAD), lambda g, qi, ki, hh: (ki[cur(g)], hh[cur(g)])),
            pl.BlockSpec((hp * B_VE, t), lambda g, qi, ki, hh: (hh[prev(g)], ki[prev(g)])),
        ],
        out_specs=pl.BlockSpec((t, hp * B_V), lambda g, qi, ki, hh: (qi[prev(g)], hh[prev(g)])),
        scratch_shapes=[
            pltpu.VMEM((hp, nchunk, t, cq), F32),
            pltpu.VMEM((hp, nchunk, 1, cq), F32),
            pltpu.VMEM((hp, nchunk, 1, cq), F32),
            pltpu.VMEM((hp, nchunk, B_VE, cq), F32),
        ],
    )
    return pl.pallas_call(
        functools.partial(_b_attn_kernel, t=t, cq=cq, kb=B_ATT_KB, kp=B_ATT_KP, hp=hp, nitems=n),
        grid_spec=grid_spec,
        out_shape=jax.ShapeDtypeStruct((s, B_HEADS * B_V), BF16),
        compiler_params=_params("arbitrary"),
        name="b_attn",
    )(qi, ki, hh, qt, k, vt)


def _b_wo_kernel(o_ref, h_ref, w_ref, out_ref):
    out_ref[...] = h_ref[...] + jnp.dot(o_ref[...], w_ref[...], preferred_element_type=F32)


def _b_wo(o, h, w_o):
    s = h.shape[0]
    tm = B_WO_TM
    return pl.pallas_call(
        _b_wo_kernel,
        grid=(s // tm,),
        in_specs=[pl.BlockSpec((tm, B_HEADS * B_V), lambda i: (i, 0)),
                  pl.BlockSpec((tm, D_MODEL), lambda i: (i, 0)),
                  pl.BlockSpec((B_HEADS * B_V, D_MODEL), lambda i: (0, 0))],
        out_specs=pl.BlockSpec((tm, D_MODEL), lambda i: (i, 0)),
        out_shape=jax.ShapeDtypeStruct((s, D_MODEL), F32),
        compiler_params=_params("parallel"),
        name="b_wo",
    )(o, h, w_o)


def _rope_freq(dim):
    return ROPE_THETA ** (-jnp.arange(0, dim, 2, dtype=F32) / dim)


def _lane_freq(dim):
    f = _rope_freq(dim)
    return jnp.concatenate([f, f, jnp.zeros((LANES - dim,), F32)])


def _angle_parts(seq_len, tile, offsets, freq):
    start = (jnp.arange(seq_len // tile, dtype=F32) * tile)[:, None] * freq[None, :]
    off = jnp.asarray(offsets, F32)[:, None] * freq[None, :]
    return jnp.cos(start), jnp.sin(start), jnp.cos(off), jnp.sin(off)


def _rope_tables(seq_len, tile, offsets, freq):
    ca, sa, cb, sb = _angle_parts(seq_len, tile, offsets, freq)
    cos = ca[:, None, :] * cb[None] - sa[:, None, :] * sb[None]
    sin = sa[:, None, :] * cb[None] + ca[:, None, :] * sb[None]
    return cos.reshape(seq_len, -1), sin.reshape(seq_len, -1)


def _rope_tables_t(seq_len, tile, freq):
    ca, sa, cb, sb = (t.T for t in _angle_parts(seq_len, tile, np.arange(tile), freq))
    cos = ca[:, :, None] * cb[:, None, :] - sa[:, :, None] * sb[:, None, :]
    sin = sa[:, :, None] * cb[:, None, :] + ca[:, :, None] * sb[:, None, :]
    return cos.reshape(-1, seq_len), sin.reshape(-1, seq_len)


def kernel(x, attn_norm_g, ffn_norm_g, a_w_qkv, a_w_o, kv_norm_g, b_w_kv_a, b_kv_a_norm_g, b_w_kv_b,
           b_w_q_a, b_q_a_norm_g, b_w_q_b, b_w_o, ffn_w_gu, ffn_w_down, final_norm_g):
    assert x.shape[0] == 1 and x.shape[2] == D_MODEL
    s = x.shape[1]
    assert s % QKV_TM == 0 and s % (A_GROUPS[-1][1] * A_ATT_TQ) == 0 and s % B_ATT_T == 0
    h = x[0]
    row = lambda g: g.reshape(1, -1).astype(F32)
    tabs_b = _rope_tables(s, B_PROJ_TM, np.arange(B_PROJ_TM), _lane_freq(B_ROPE))
    tabs_bt = _rope_tables_t(s, B_PROJ_TM, _rope_freq(B_ROPE))

    w_qkv = a_w_qkv[0].astype(BF16)
    outs, maxes, dens = [], [], []
    for group, (_, d) in enumerate(A_GROUPS):
        qkv = _a_qkv(h, row(attn_norm_g[0]), w_qkv, group, d)
        o, row_max, den = _a_attn(qkv, d)
        outs.append(o)
        maxes.append(row_max)
        dens.append(den)
    h = _a_wo(outs, maxes, dens, h, a_w_o[0].astype(BF16))
    w_gu, w_down = ffn_w_gu.astype(BF16), ffn_w_down.astype(BF16)
    h = _ffn(h, row(ffn_norm_g[0]), w_gu, w_down, 0, row(final_norm_g), final_norm=False)

    w_kv_a = jnp.pad(b_w_kv_a, ((0, 0), (0, LANES - B_ROPE))).astype(BF16)
    qt, k, vt = _b_proj(h, row(attn_norm_g[1]), row(kv_norm_g), b_w_q_a[0].astype(BF16), w_kv_a,
                        row(b_q_a_norm_g[0]), row(b_kv_a_norm_g), b_w_q_b[0].T.astype(BF16),
                        b_w_kv_b.astype(BF16), tabs_b, tabs_bt)
    o = _b_attn(qt, k, vt)
    h = _b_wo(o, h, b_w_o[0].astype(BF16))
    h = _ffn(h, row(ffn_norm_g[1]), w_gu, w_down, 1, row(final_norm_g), final_norm=True)
    return h[None]
```
